```python
import numpy as np
import jax
import jax.numpy as jnp
from jax import lax

D_MODEL = 2048
BATCH = 8
SEQ = 4096
DEPTH = 2

D_MIX = D_MODEL
MOBA_HEADS = 4
MOBA_HEAD_DIM = D_MIX // 16
MOBA_W = MOBA_HEADS * MOBA_HEAD_DIM
MOBA_BLOCK = 256
MOBA_TOPK = 3
MOBA_Q_CHUNK = 32
RWKV_HEADS = 8
RWKV_HEAD_DIM = D_MIX // 32
RWKV_W = RWKV_HEADS * RWKV_HEAD_DIM
RWKV_W_LORA = 64
RWKV_A_LORA = 64
RWKV_G_LORA = 128
RWKV_GN_EPS = 64e-5
FOX_HEADS = 4
FOX_HEAD_DIM = D_MIX // 16
FOX_W = FOX_HEADS * FOX_HEAD_DIM
FOX_Q_BLOCK = 128
RET_HEADS = 4
RET_KEY_DIM = D_MIX // 32
RET_VALUE_DIM = D_MIX // 16
RET_W = RET_HEADS * RET_VALUE_DIM
RET_CHUNK = 128
RET_GN_EPS = 1e-5
MOBA_COLS = 3 * MOBA_W
RWKV_SHIFT_COLS = 3 * RWKV_W + RWKV_W_LORA + RWKV_A_LORA + RWKV_G_LORA
FOX_COLS = 3 * FOX_W + FOX_HEADS
RET_COLS = 2 * RET_HEADS * RET_KEY_DIM + 2 * RET_W
C_IN = MOBA_COLS + RWKV_SHIFT_COLS + FOX_COLS + RET_COLS
N_EXPERTS = 32
TOP_K = 4
D_EXPERT = D_MODEL
SWIGLU_ALPHA = 1.702
SWIGLU_LIMIT = 7.0
LN_EPS = 1e-5
DEEPNORM_ALPHA = (2 * DEPTH) ** 0.25
DEEPNORM_BETA = (8 * DEPTH) ** -0.25

kernel_name = 'hybrid_parallel_heads_moe_block'

F32 = jnp.float32


def _split(t, sizes):
    return jnp.split(t, np.cumsum(sizes)[:-1].tolist(), axis=-1)


def _heads(t, n_heads):
    B, S, _ = t.shape
    return t.reshape(B, S, n_heads, -1).transpose(0, 2, 1, 3)


def _merge(t):
    B, H, S, d = t.shape
    return t.transpose(0, 2, 1, 3).reshape(B, S, H * d)


def _token_shift(t):
    return jnp.pad(t[:, :-1], ((0, 0), (1, 0), (0, 0)))


def _layer_norm(x, g, b):
    xf = x.astype(F32)
    mu = xf.mean(-1, keepdims=True)
    var = jnp.square(xf - mu).mean(-1, keepdims=True)
    return ((xf - mu) * lax.rsqrt(var + LN_EPS) * g + b).astype(x.dtype)


def _group_norm(y, g, b, eps):
    B, S, H, N = y.shape
    y = y.astype(F32)
    mu = y.mean(-1, keepdims=True)
    var = jnp.square(y - mu).mean(-1, keepdims=True)
    return ((y - mu) * lax.rsqrt(var + eps)).reshape(B, S, H * N) * g + b


def _alibi_slopes(n_heads):
    return jnp.exp2(-8.0 * jnp.arange(1, n_heads + 1, dtype=F32) / n_heads)


def _moba_attention(q, k, v):
    B, H, S, d = q.shape
    L, QC = MOBA_BLOCK, MOBA_Q_CHUNK
    nb = -(-S // L)
    pad = ((0, 0), (0, 0), (0, nb * L - S), (0, 0))
    kb = jnp.pad(k, pad).reshape(B, H, nb, L, d)
    vb = jnp.pad(v, pad).reshape(B, H, nb, L, d)
    q_blk = jnp.arange(S) // L
    fully_past = jnp.arange(nb)[None, :] < q_blk[:, None]
    gate = jnp.einsum('bhsd,bhnd->bhsn', q, kb.mean(axis=3)).astype(F32)
    gate = jnp.where(fully_past, gate, -jnp.inf)
    n_sel = min(MOBA_TOPK, nb)
    _, sel = lax.top_k(gate, n_sel)
    slopes = _alibi_slopes(H)[None, :, None, None]
    scale = d ** -0.5
    nc = S // QC
    q_chunks = q.reshape(B, H, nc, QC, d).transpose(2, 0, 1, 3, 4)
    sel_chunks = sel.reshape(B, H, nc, QC, n_sel).transpose(2, 0, 1, 3, 4)
    gather_blocks = jax.vmap(jax.vmap(lambda blocks, idx: blocks[idx]))
    offs = jnp.arange(L)
    T = n_sel * L

    def chunk(args):
        q_i, sel_i, c = args
        q_pos = c * QC + jnp.arange(QC)
        own = (c * QC) // L
        k_sel = gather_blocks(kb, sel_i)
        v_sel = gather_blocks(vb, sel_i)
        dist_sel = (q_pos[:, None, None] - (sel_i[..., None] * L + offs)).astype(F32)
        s_sel = jnp.einsum('bhqd,bhqtld->bhqtl', q_i, k_sel).astype(F32) * scale - slopes[..., None] * dist_sel
        s_sel = jnp.where((sel_i < own)[..., None], s_sel, -jnp.inf)
        k_own = lax.dynamic_index_in_dim(kb, own, axis=2, keepdims=False)
        v_own = lax.dynamic_index_in_dim(vb, own, axis=2, keepdims=False)
        dist_own = (q_pos[:, None] - (own * L + offs)[None, :]).astype(F32)
        s_own = jnp.einsum('bhqd,bhld->bhql', q_i, k_own).astype(F32) * scale - slopes * dist_own
        s_own = jnp.where(dist_own >= 0, s_own, -jnp.inf)
        p = jax.nn.softmax(jnp.concatenate([s_sel.reshape(B, H, QC, T), s_own], axis=-1), axis=-1).astype(v.dtype)
        o_sel = jnp.einsum('bhqtl,bhqtld->bhqd', p[..., :T].reshape(B, H, QC, n_sel, L), v_sel)
        o_own = jnp.einsum('bhql,bhld->bhqd', p[..., T:], v_own)
        return o_sel + o_own

    out = lax.map(chunk, (q_chunks, sel_chunks, jnp.arange(nc)))
    return out.transpose(1, 2, 0, 3, 4).reshape(B, H, S, d)


def _rwkv7_scan(r, w, k, v, kk, a):
    B, S, H, N = r.shape

    def step(state, inp):
        r_t, w_t, k_t, v_t, kk_t, a_t = inp
        sa = jnp.einsum('bhij,bhj->bhi', state, -kk_t)
        state = (state * w_t[:, :, None, :] + sa[..., None] * (kk_t * a_t)[:, :, None, :]
                 + v_t[..., None] * k_t[:, :, None, :])
        return state, jnp.einsum('bhij,bhj->bhi', state, r_t)

    xs = tuple(jnp.swapaxes(t.astype(F32), 0, 1) for t in (r, w, k, v, kk, a))
    _, y = lax.scan(step, jnp.zeros((B, H, N, N), F32), xs)
    return jnp.swapaxes(y, 0, 1)


def _rwkv7_time_mix(cols, w0, w_up, a0, a_up, g_up, k_k, k_a, r_k, ln_g, ln_b):
    B, S, _ = cols.shape
    H, N = RWKV_HEADS, RWKV_HEAD_DIM
    cols = cols.astype(F32)
    r, k, v, wd, ad, gd = _split(cols, (RWKV_W, RWKV_W, RWKV_W, RWKV_W_LORA, RWKV_A_LORA, RWKV_G_LORA))
    w_log = -jax.nn.softplus(-(w0 + jnp.tanh(wd) @ w_up)) - 0.5
    decay = jnp.exp(-jnp.exp(w_log))
    a = jax.nn.sigmoid(a0 + ad @ a_up)
    g = jax.nn.sigmoid(gd) @ g_up
    kk = (k * k_k).reshape(B, S, H, N)
    kk = kk / jnp.maximum(jnp.linalg.norm(kk, axis=-1, keepdims=True), 1e-12)
    k = k * (1.0 + (a - 1.0) * k_a)
    rh, kh, vh, ah, wh = (t.reshape(B, S, H, N) for t in (r, k, v, a, decay))
    y = _rwkv7_scan(rh, wh, kh, vh, kk, ah)
    y = _group_norm(y, ln_g, ln_b, RWKV_GN_EPS)
    bonus = (jnp.sum(rh * kh * r_k, axis=-1, keepdims=True) * vh).reshape(B, S, RWKV_W)
    return (y + bonus) * g


def _forgetting_attention(q, k, v, log_f):
    B, H, S, d = q.shape
    QB = FOX_Q_BLOCK
    nq = S // QB
    scale = d ** -0.5
    c = jnp.cumsum(log_f, axis=-1)
    k_pos = jnp.arange(S)
    q_blocks = q.reshape(B, H, nq, QB, d).transpose(2, 0, 1, 3, 4)
    c_blocks = c.reshape(B, H, nq, QB).transpose(2, 0, 1, 3)

    def block(args):
        q_i, c_i, i = args
        q_pos = i * QB + jnp.arange(QB)
        s = jnp.einsum('bhqd,bhkd->bhqk', q_i, k).astype(F32) * scale + c_i[..., None] - c[:, :, None, :]
        s = jnp.where(k_pos[None, :] <= q_pos[:, None], s, -jnp.inf)
        p = jax.nn.softmax(s, axis=-1).astype(v.dtype)
        return jnp.einsum('bhqk,bhkd->bhqd', p, v)

    out = lax.map(block, (q_blocks, c_blocks, jnp.arange(nq)))
    return out.transpose(1, 2, 0, 3, 4).reshape(B, H, S, d)


def _retention_chunkwise(q, k, v):
    B, H, S, dk = q.shape
    dv = v.shape[-1]
    C = RET_CHUNK
    nc = S // C
    log_gamma = jnp.log(1.0 - jnp.exp2(-5.0 - jnp.arange(H, dtype=F32)))
    idx = jnp.arange(C, dtype=F32)
    diff = idx[:, None] - idx[None, :]
    intra = jnp.where(diff >= 0, jnp.exp(log_gamma[:, None, None] * jnp.maximum(diff, 0.0)), 0.0)
    q_dec = jnp.exp(log_gamma[:, None] * (idx + 1.0))[..., None]
    k_dec = jnp.exp(log_gamma[:, None] * (C - 1.0 - idx))[..., None]
    blk_dec = jnp.exp(log_gamma * C)[:, None, None]

    def chunks(t):
        return jnp.moveaxis(t.astype(F32).reshape(B, H, nc, C, t.shape[-1]), 2, 0)

    def step(R, inp):
        q_i, k_i, v_i = inp
        inner = jnp.einsum('bhqd,bhkd->bhqk', q_i, k_i) * intra
        o = jnp.einsum('bhqk,bhkv->bhqv', inner, v_i) + jnp.einsum('bhqd,bhdv->bhqv', q_i, R) * q_dec
        R = R * blk_dec + jnp.einsum('bhkd,bhkv->bhdv', k_i * k_dec, v_i)
        return R, o

    _, o = lax.scan(step, jnp.zeros((B, H, dk, dv), F32), (chunks(q), chunks(k * dk ** -0.5), chunks(v)))
    return jnp.moveaxis(o, 0, 2).reshape(B, H, S, dv)


def _hybrid_mixer(x, w_in, rwkv_mu, rwkv_w0, rwkv_w_up, rwkv_a0, rwkv_a_up, rwkv_g_up, rwkv_k_k,
                  rwkv_k_a, rwkv_r_k, rwkv_ln_g, rwkv_ln_b, fox_b_f, ret_ln_g, ret_ln_b, w_out):
    proj = x @ w_in
    moba_cols, rwkv_cols, fox_cols, ret_cols = _split(proj, (MOBA_COLS, RWKV_SHIFT_COLS, FOX_COLS, RET_COLS))
    mq, mk, mv = _split(moba_cols, (MOBA_W, MOBA_W, MOBA_W))
    y_a = _merge(_moba_attention(_heads(mq, MOBA_HEADS), _heads(mk, MOBA_HEADS), _heads(mv, MOBA_HEADS)))
    rwkv_cols = rwkv_cols + (_token_shift(rwkv_cols) - rwkv_cols) * rwkv_mu
    y_b = _rwkv7_time_mix(rwkv_cols, rwkv_w0, rwkv_w_up, rwkv_a0, rwkv_a_up, rwkv_g_up, rwkv_k_k,
                          rwkv_k_a, rwkv_r_k, rwkv_ln_g, rwkv_ln_b)
    fq, fk, fv, f_logit = _split(fox_cols, (FOX_W, FOX_W, FOX_W, FOX_HEADS))
    log_f = jax.nn.log_sigmoid((f_logit + fox_b_f).astype(F32)).transpose(0, 2, 1)
    y_c = _merge(_forgetting_attention(_heads(fq, FOX_HEADS), _heads(fk, FOX_HEADS), _heads(fv, FOX_HEADS), log_f))
    rq, rk, rv, rg = _split(ret_cols, (RET_HEADS * RET_KEY_DIM, RET_HEADS * RET_KEY_DIM, RET_W, RET_W))
    ret = _retention_chunkwise(_heads(rq, RET_HEADS), _heads(rk, RET_HEADS), _heads(rv, RET_HEADS))
    y_d = jax.nn.silu(rg.astype(F32)) * _group_norm(ret.transpose(0, 2, 1, 3), ret_ln_g, ret_ln_b, RET_GN_EPS)
    y = jnp.concatenate([y_a.astype(F32), y_b, y_c.astype(F32), y_d], axis=-1).astype(x.dtype)
    return y @ w_out


def _moe(h, router_w, router_b, w1, b1, w2, b2):
    B, S, D = h.shape
    t = h.reshape(B * S, D)
    logits = (t @ router_w + router_b).astype(F32)
    top_val, top_idx = lax.top_k(logits, TOP_K)
    gates = jax.nn.softmax(top_val, axis=-1)
    gate_dense = jnp.einsum('nk,nke->ne', gates, jax.nn.one_hot(top_idx, N_EXPERTS, dtype=F32))
    out = jnp.zeros((B * S, D), F32)
    for e in range(N_EXPERTS):
        u = t @ w1[e] + b1[e]
        glu = jnp.minimum(u[:, :D_EXPERT], SWIGLU_LIMIT)
        lin = jnp.clip(u[:, D_EXPERT:], -SWIGLU_LIMIT, SWIGLU_LIMIT)
        act = glu * jax.nn.sigmoid(SWIGLU_ALPHA * glu) * (lin + 1.0)
        out = out + gate_dense[:, e:e + 1] * (act @ w2[e] + b2[e])
    return out.reshape(B, S, D).astype(h.dtype)


def setup_inputs(seed: int = 0) -> dict:
    key = jax.random.key(seed)
    ks = jax.random.split(key, 32)
    L = DEPTH

    def nrm(k, shape, s):
        return jax.random.normal(k, shape, F32) * s

    def uni(k, shape, lo, hi):
        return jax.random.uniform(k, shape, F32, lo, hi)

    return {
        'x': nrm(ks[0], (BATCH, SEQ, D_MODEL), 1.0),
        'w_in': nrm(ks[1], (L, D_MODEL, C_IN), D_MODEL ** -0.5),
        'rwkv_mu': uni(ks[2], (L, RWKV_SHIFT_COLS), 0.0, 1.0),
        'rwkv_w0': uni(ks[3], (L, RWKV_W), -6.0, -1.0),
        'rwkv_w_up': nrm(ks[4], (L, RWKV_W_LORA, RWKV_W), 0.1 * RWKV_W_LORA ** -0.5),
        'rwkv_a0': nrm(ks[5], (L, RWKV_W), 0.1),
        'rwkv_a_up': nrm(ks[6], (L, RWKV_A_LORA, RWKV_W), 0.1 * RWKV_A_LORA ** -0.5),
        'rwkv_g_up': nrm(ks[7], (L, RWKV_G_LORA, RWKV_W), RWKV_G_LORA ** -0.5),
        'rwkv_k_k': 0.85 + nrm(ks[8], (L, RWKV_W), 0.02),
        'rwkv_k_a': 1.0 + nrm(ks[9], (L, RWKV_W), 0.02),
        'rwkv_r_k': nrm(ks[10], (L, RWKV_HEADS, RWKV_HEAD_DIM), 0.1),
        'rwkv_ln_g': 1.0 + nrm(ks[11], (L, RWKV_W), 0.02),
        'rwkv_ln_b': nrm(ks[12], (L, RWKV_W), 0.02),
        'fox_b_f': uni(ks[13], (L, FOX_HEADS), 1.0, 5.0),
        'ret_ln_g': 1.0 + nrm(ks[14], (L, RET_W), 0.02),
        'ret_ln_b': nrm(ks[15], (L, RET_W), 0.02),
        'w_out': nrm(ks[16], (L, D_MIX, D_MODEL), D_MIX ** -0.5 * DEEPNORM_BETA),
        'ln1_g': 1.0 + nrm(ks[17], (L, D_MODEL), 0.02),
        'ln1_b': nrm(ks[18], (L, D_MODEL), 0.02),
        'router_w': nrm(ks[19], (L, D_MODEL, N_EXPERTS), D_MODEL ** -0.5),
        'router_b': nrm(ks[20], (L, N_EXPERTS), 0.01),
        'exp_w1': nrm(ks[21], (L, N_EXPERTS, D_MODEL, 2 * D_EXPERT), D_MODEL ** -0.5),
        'exp_b1': nrm(ks[22], (L, N_EXPERTS, 2 * D_EXPERT), 0.01),
        'exp_w2': nrm(ks[23], (L, N_EXPERTS, D_EXPERT, D_MODEL), D_EXPERT ** -0.5 * DEEPNORM_BETA),
        'exp_b2': nrm(ks[24], (L, N_EXPERTS, D_MODEL), 0.01),
        'ln2_g': 1.0 + nrm(ks[25], (L, D_MODEL), 0.02),
        'ln2_b': nrm(ks[26], (L, D_MODEL), 0.02),
    }


def reference(x, w_in, rwkv_mu, rwkv_w0, rwkv_w_up, rwkv_a0, rwkv_a_up, rwkv_g_up, rwkv_k_k, rwkv_k_a,
              rwkv_r_k, rwkv_ln_g, rwkv_ln_b, fox_b_f, ret_ln_g, ret_ln_b, w_out, ln1_g, ln1_b,
              router_w, router_b, exp_w1, exp_b1, exp_w2, exp_b2, ln2_g, ln2_b):
    h = x
    for l in range(DEPTH):
        mix = _hybrid_mixer(h, w_in[l], rwkv_mu[l], rwkv_w0[l], rwkv_w_up[l], rwkv_a0[l], rwkv_a_up[l],
                            rwkv_g_up[l], rwkv_k_k[l], rwkv_k_a[l], rwkv_r_k[l], rwkv_ln_g[l], rwkv_ln_b[l],
                            fox_b_f[l], ret_ln_g[l], ret_ln_b[l], w_out[l])
        h = _layer_norm(DEEPNORM_ALPHA * h + mix, ln1_g[l], ln1_b[l])
        ffn = _moe(h, router_w[l], router_b[l], exp_w1[l], exp_b1[l], exp_w2[l], exp_b2[l])
        h = _layer_norm(DEEPNORM_ALPHA * h + ffn, ln2_g[l], ln2_b[l])
    return h
```

```python
import functools

import numpy as np
import jax
import jax.numpy as jnp
from jax import lax
from jax.experimental import pallas as pl
from jax.experimental.pallas import tpu as pltpu

F32 = jnp.float32
BF16 = jnp.bfloat16
I32 = jnp.int32
U32 = jnp.uint32
HI = lax.Precision.HIGHEST

D_MODEL = 2048
DEPTH = 2
MOBA_HEADS = 4
MOBA_HEAD_DIM = 128
MOBA_W = MOBA_HEADS * MOBA_HEAD_DIM
MOBA_BLOCK = 256
MOBA_TOPK = 3
RWKV_HEADS = 8
RWKV_HEAD_DIM = 64
RWKV_W = RWKV_HEADS * RWKV_HEAD_DIM
RWKV_W_LORA = 64
RWKV_A_LORA = 64
RWKV_G_LORA = 128
RWKV_GN_EPS = 64e-5
RWKV_CHUNK = 64
FOX_HEADS = 4
FOX_HEAD_DIM = 128
FOX_W = FOX_HEADS * FOX_HEAD_DIM
RET_HEADS = 4
RET_KEY_DIM = 64
RET_VALUE_DIM = 128
RET_W = RET_HEADS * RET_VALUE_DIM
RET_GN_EPS = 1e-5
MOBA_COLS = 3 * MOBA_W
RWKV_SHIFT_COLS = 3 * RWKV_W + RWKV_W_LORA + RWKV_A_LORA + RWKV_G_LORA
FOX_COLS = 3 * FOX_W + FOX_HEADS
RET_COLS = 2 * RET_HEADS * RET_KEY_DIM + 2 * RET_W
N_EXPERTS = 32
TOP_K = 4
D_EXPERT = D_MODEL
SWIGLU_ALPHA = 1.702
SWIGLU_LIMIT = 7.0
LN_EPS = 1e-5
DEEPNORM_ALPHA = (2 * DEPTH) ** 0.25

LANES = 128
V7X_VMEM_LIMIT = 56 * 1024 * 1024

_NT = (((1,), (1,)), ((), ()))
_TN = (((0,), (0,)), ((), ()))


def _dot(a, b, dims=None, precision=None):
    if dims is None:
        dims = (((a.ndim - 1,), (0,)), ((), ()))
    return lax.dot_general(a, b, dims, precision=precision, preferred_element_type=F32)


def _sigmoid(x):
    return 1.0 / (1.0 + jnp.exp(-x))


def _softplus(x):
    return jnp.maximum(x, 0.0) + jnp.log(1.0 + jnp.exp(-jnp.abs(x)))


def _params(sem, vmem=None):
    return pltpu.CompilerParams(dimension_semantics=sem, vmem_limit_bytes=vmem)


def _mm_kernel(x_ref, w_ref, o_ref, xb_ref):
    @pl.when(pl.program_id(1) == 0)
    def _():
        xb_ref[...] = x_ref[...].astype(BF16)

    o_ref[...] = _dot(xb_ref[...], w_ref[...]).astype(o_ref.dtype)


def _matmul(x, w, out_dtype, tm, tn):
    T, K = x.shape
    N = w.shape[1]
    assert T % tm == 0 and N % tn == 0
    return pl.pallas_call(
        _mm_kernel,
        grid=(T // tm, N // tn),
        in_specs=[pl.BlockSpec((tm, K), lambda i, j: (i, 0)), pl.BlockSpec((K, tn), lambda i, j: (0, j))],
        out_specs=pl.BlockSpec((tm, tn), lambda i, j: (i, j)),
        out_shape=jax.ShapeDtypeStruct((T, N), out_dtype),
        scratch_shapes=[pltpu.VMEM((tm, K), BF16)],
        compiler_params=_params(("parallel", "arbitrary"), V7X_VMEM_LIMIT),
        name="in_proj",
    )(x, w)


def _topk_lanes(vals, lane_f, k):
    outs = []
    g = vals
    for _ in range(k):
        m = jnp.max(g, axis=1, keepdims=True)
        cand = jnp.where((g == m) & (g > -jnp.inf), lane_f, float(LANES))
        idx = jnp.min(cand, axis=1, keepdims=True)
        outs.append((m, idx))
        g = jnp.where(lane_f == idx, -jnp.inf, g)
    return outs


def _moba_kernel(slopes_ref, q_ref, k_ref, v_ref, o_ref, kmean_ref):
    L = MOBA_BLOCK
    S = k_ref.shape[1]
    nb = S // L
    h = pl.program_id(1)
    i = pl.program_id(2)
    scale = MOBA_HEAD_DIM ** -0.5

    @pl.when(i == 0)
    def _():
        kmean_ref[...] = jnp.zeros_like(kmean_ref)
        for n in range(nb):
            kb = k_ref[0, n * L:(n + 1) * L, :].astype(F32)
            kmean_ref[n:n + 1, :] = jnp.sum(kb, axis=0, keepdims=True) * (1.0 / L)

    q = q_ref[0]
    lane = lax.broadcasted_iota(I32, (L, LANES), 1)
    lane_f = lane.astype(F32)
    gate = _dot(q.astype(F32), kmean_ref[...], _NT, precision=HI)
    gate = jnp.where(lane < i, gate, -jnp.inf)
    sel = jnp.zeros((L, LANES), F32)
    for _, idx in _topk_lanes(gate, lane_f, MOBA_TOPK):
        sel = jnp.where(lane_f == idx, 1.0, sel)

    slope = slopes_ref[h]
    rc = (lax.broadcasted_iota(I32, (L, L), 0) - lax.broadcasted_iota(I32, (L, L), 1)).astype(F32)

    def scores(kblk, dist_off):
        return _dot(q, kblk, _NT) * scale - slope * (rc + dist_off)

    own = pl.multiple_of(i * L, L)
    s = jnp.where(rc >= 0, scores(k_ref[0, pl.ds(own, L), :], 0.0), -jnp.inf)
    m0 = jnp.max(s, axis=1, keepdims=True)
    p = jnp.exp(s - m0)
    l0 = jnp.sum(p, axis=1, keepdims=True)
    acc0 = _dot(p.astype(BF16), v_ref[0, pl.ds(own, L), :])

    def body(n, carry):
        m, l, acc = carry
        start = pl.multiple_of(n * L, L)
        selcol = jnp.max(jnp.where(lane == n, sel, 0.0), axis=1, keepdims=True)
        s = scores(k_ref[0, pl.ds(start, L), :], ((i - n) * L).astype(F32))
        s = jnp.where(selcol > 0.0, s, -jnp.inf)
        m_new = jnp.maximum(m, jnp.max(s, axis=1, keepdims=True))
        alpha = jnp.exp(m - m_new)
        p = jnp.exp(s - m_new)
        l = alpha * l + jnp.sum(p, axis=1, keepdims=True)
        acc = alpha * acc + _dot(p.astype(BF16), v_ref[0, pl.ds(start, L), :])
        return m_new, l, acc

    _, l, acc = lax.fori_loop(0, i, body, (m0, l0, acc0))
    o_ref[0] = (acc / l).astype(o_ref.dtype)


def _moba(qkv, slopes):
    B, S, _ = qkv.shape
    H, L, d = MOBA_HEADS, MOBA_BLOCK, MOBA_HEAD_DIM
    assert S % L == 0 and S // L <= LANES
    return pl.pallas_call(
        _moba_kernel,
        grid_spec=pltpu.PrefetchScalarGridSpec(
            num_scalar_prefetch=1,
            grid=(B, H, S // L),
            in_specs=[
                pl.BlockSpec((1, L, d), lambda b, h, i, sl: (b, i, h)),
                pl.BlockSpec((1, S, d), lambda b, h, i, sl: (b, 0, H + h)),
                pl.BlockSpec((1, S, d), lambda b, h, i, sl: (b, 0, 2 * H + h)),
            ],
            out_specs=pl.BlockSpec((1, L, d), lambda b, h, i, sl: (b, i, h)),
            scratch_shapes=[pltpu.VMEM((LANES, d), F32)],
        ),
        out_shape=jax.ShapeDtypeStruct((B, S, H * d), BF16),
        compiler_params=_params(("parallel", "parallel", "arbitrary")),
        name="moba",
    )(slopes, qkv, qkv, qkv)


def _fox_prep_kernel(f_ref, bf_ref, c_ref, carry_ref):
    ts = f_ref.shape[1]

    @pl.when(pl.program_id(1) == 0)
    def _():
        carry_ref[...] = jnp.zeros_like(carry_ref)

    z = f_ref[0] + bf_ref[...]
    lf = -_softplus(-z)
    tri = (lax.broadcasted_iota(I32, (ts, ts), 0) >= lax.broadcasted_iota(I32, (ts, ts), 1)).astype(F32)
    c = _dot(tri, lf, precision=HI) + carry_ref[0:1, :]
    c_ref[0] = c
    carry_ref[0:1, :] = c[ts - 1:ts, :]


def _fox_prep(f_logit, b_f, ts):
    B, S, W = f_logit.shape
    return pl.pallas_call(
        _fox_prep_kernel,
        grid=(B, S // ts),
        in_specs=[pl.BlockSpec((1, ts, W), lambda b, j: (b, j, 0)), pl.BlockSpec((1, W), lambda b, j: (0, 0))],
        out_specs=pl.BlockSpec((1, ts, W), lambda b, j: (b, j, 0)),
        out_shape=jax.ShapeDtypeStruct((B, S, W), F32),
        scratch_shapes=[pltpu.VMEM((8, W), F32)],
        compiler_params=_params(("parallel", "arbitrary")),
        name="fox_prep",
    )(f_logit, b_f)


def _fox_kernel(q_ref, k_ref, v_ref, ccol_ref, crow_ref, o_ref):
    tq = q_ref.shape[1]
    i = pl.program_id(2)
    scale = FOX_HEAD_DIM ** -0.5
    q = q_ref[0]
    cq = ccol_ref[0, 0]
    rc = lax.broadcasted_iota(I32, (tq, tq), 0) - lax.broadcasted_iota(I32, (tq, tq), 1)

    def scores(n):
        start = pl.multiple_of(n * tq, tq)
        s = _dot(q, k_ref[0, pl.ds(start, tq), :], _NT) * scale + cq - crow_ref[0, 0, n]
        return s, v_ref[0, pl.ds(start, tq), :]

    s, vblk = scores(i)
    s = jnp.where(rc >= 0, s, -jnp.inf)
    m0 = jnp.max(s, axis=1, keepdims=True)
    p = jnp.exp(s - m0)
    l0 = jnp.sum(p, axis=1, keepdims=True)
    acc0 = _dot(p.astype(BF16), vblk)

    def body(n, carry):
        m, l, acc = carry
        s, vblk = scores(n)
        m_new = jnp.maximum(m, jnp.max(s, axis=1, keepdims=True))
        alpha = jnp.exp(m - m_new)
        p = jnp.exp(s - m_new)
        l = alpha * l + jnp.sum(p, axis=1, keepdims=True)
        acc = alpha * acc + _dot(p.astype(BF16), vblk)
        return m_new, l, acc

    _, l, acc = lax.fori_loop(0, i, body, (m0, l0, acc0))
    o_ref[0] = (acc / l).astype(o_ref.dtype)


def _fox(qkv, c, tq):
    B, S, _ = qkv.shape
    H, d = FOX_HEADS, FOX_HEAD_DIM
    nq = S // tq
    ch = jnp.transpose(c[:, :, :H], (0, 2, 1))
    c_col = ch[..., None]
    c_row = ch.reshape(B, H, nq, 1, tq)
    return pl.pallas_call(
        _fox_kernel,
        grid=(B, H, nq),
        in_specs=[
            pl.BlockSpec((1, tq, d), lambda b, h, i: (b, i, h)),
            pl.BlockSpec((1, S, d), lambda b, h, i: (b, 0, H + h)),
            pl.BlockSpec((1, S, d), lambda b, h, i: (b, 0, 2 * H + h)),
            pl.BlockSpec((1, 1, tq, 1), lambda b, h, i: (b, h, i, 0)),
            pl.BlockSpec((1, 1, nq, 1, tq), lambda b, h, i: (b, h, 0, 0, 0)),
        ],
        out_specs=pl.BlockSpec((1, tq, d), lambda b, h, i: (b, i, h)),
        out_shape=jax.ShapeDtypeStruct((B, S, H * d), BF16),
        compiler_params=_params(("parallel", "parallel", "arbitrary")),
        name="fox",
    )(qkv, qkv, qkv, c_col, c_row)


def _ret_kernel(q_ref, k_ref, v_ref, g_ref, lng_ref, lnb_ref, o_ref, state_ref):
    C = q_ref.shape[1]
    dk, dv = RET_KEY_DIM, RET_VALUE_DIM

    @pl.when(pl.program_id(1) == 0)
    def _():
        state_ref[...] = jnp.zeros_like(state_ref)

    row = lax.broadcasted_iota(I32, (C, C), 0)
    col = lax.broadcasted_iota(I32, (C, C), 1)
    diff = (row - col).astype(F32)
    tcol = lax.broadcasted_iota(I32, (C, 1), 0).astype(F32)
    scale = dk ** -0.5
    for h in range(RET_HEADS):
        log_gamma = float(np.log(1.0 - 2.0 ** (-5.0 - h)))
        intra = jnp.where(diff >= 0, jnp.exp(log_gamma * jnp.maximum(diff, 0.0)), 0.0)
        q_dec = jnp.exp(log_gamma * (tcol + 1.0))
        k_dec = jnp.exp(log_gamma * (C - 1.0 - tcol))
        blk_dec = float(np.exp(log_gamma * C))
        qh = q_ref[0, :, h * dk:(h + 1) * dk]
        kh = (k_ref[0, :, h * dk:(h + 1) * dk].astype(F32) * scale).astype(BF16)
        vh = v_ref[0, :, h * dv:(h + 1) * dv]
        R = state_ref[h]
        inner = _dot(qh, kh, _NT) * intra
        o = _dot(inner.astype(BF16), vh) + _dot(qh, R.astype(BF16)) * q_dec
        state_ref[h] = R * blk_dec + _dot((kh.astype(F32) * k_dec).astype(BF16), vh, _TN)
        mu = jnp.mean(o, axis=1, keepdims=True)
        d = o - mu
        var = jnp.mean(d * d, axis=1, keepdims=True)
        y = d * lax.rsqrt(var + RET_GN_EPS) * lng_ref[:, h * dv:(h + 1) * dv] + lnb_ref[:, h * dv:(h + 1) * dv]
        g = g_ref[0, :, h * dv:(h + 1) * dv].astype(F32)
        o_ref[0, :, h * dv:(h + 1) * dv] = (g * _sigmoid(g) * y).astype(o_ref.dtype)


def _retention(ret, ln_g, ln_b, C):
    B, S, _ = ret.shape
    qk = RET_HEADS * RET_KEY_DIM
    return pl.pallas_call(
        _ret_kernel,
        grid=(B, S // C),
        in_specs=[
            pl.BlockSpec((1, C, qk), lambda b, c: (b, c, 0)),
            pl.BlockSpec((1, C, qk), lambda b, c: (b, c, 1)),
            pl.BlockSpec((1, C, RET_W), lambda b, c: (b, c, 1)),
            pl.BlockSpec((1, C, RET_W), lambda b, c: (b, c, 2)),
            pl.BlockSpec((1, RET_W), lambda b, c: (0, 0)),
            pl.BlockSpec((1, RET_W), lambda b, c: (0, 0)),
        ],
        out_specs=pl.BlockSpec((1, C, RET_W), lambda b, c: (b, c, 0)),
        out_shape=jax.ShapeDtypeStruct((B, S, RET_W), BF16),
        scratch_shapes=[pltpu.VMEM((RET_HEADS, RET_KEY_DIM, RET_VALUE_DIM), F32)],
        compiler_params=_params(("parallel", "arbitrary")),
        name="retention",
    )(ret, ret, ret, ret, ln_g, ln_b)


def _rwkv_prep_kernel(x_ref, mu_ref, w0_ref, wup_ref, a0_ref, aup_ref, gup_ref, kk_ref, ka_ref, rk_ref, hsum_ref,
                      r_o, lw_o, k_o, v_o, a_o, b_o, g_o, bonus_o, carry_ref):
    ts = x_ref.shape[1]
    W = RWKV_W

    @pl.when(pl.program_id(1) == 0)
    def _():
        carry_ref[...] = jnp.zeros_like(carry_ref)

    x = x_ref[0]
    row = lax.broadcasted_iota(I32, x.shape, 0)
    shifted = jnp.where(row == 0, carry_ref[0:1, :], pltpu.roll(x, 1, 0))
    carry_ref[0:1, :] = x[ts - 1:ts, :]
    xs = x + (shifted - x) * mu_ref[...]
    r = xs[:, 0:W]
    k = xs[:, W:2 * W]
    v = xs[:, 2 * W:3 * W]
    wd = xs[:, 3 * W:3 * W + LANES]
    ad = xs[:, 3 * W + LANES:3 * W + 2 * LANES]
    gd = xs[:, 3 * W + 2 * LANES:3 * W + 3 * LANES]
    w_log = -_softplus(-(w0_ref[...] + _dot(jnp.tanh(wd), wup_ref[...], precision=HI))) - 0.5
    a = _sigmoid(a0_ref[...] + _dot(ad, aup_ref[...], precision=HI))
    g = _dot(_sigmoid(gd), gup_ref[...], precision=HI)
    kk = k * kk_ref[...]
    norm = jnp.sqrt(_dot(kk * kk, hsum_ref[...], precision=HI))
    kk = kk / jnp.maximum(norm, 1e-12)
    k2 = k * (1.0 + (a - 1.0) * ka_ref[...])
    r_o[0] = r
    lw_o[0] = -jnp.exp(w_log)
    k_o[0] = k2
    v_o[0] = v
    a_o[0] = -kk
    b_o[0] = kk * a
    g_o[0] = g
    bonus_o[0] = _dot(r * k2 * rk_ref[...], hsum_ref[...], precision=HI) * v


def _rwkv_prep(cols, mu, w0, wup, a0, aup, gup, k_k, k_a, r_k, hsum, ts):
    B, S, Wc = cols.shape
    W = RWKV_W
    vec = lambda n: pl.BlockSpec((1, n), lambda b, j: (0, 0))
    mat = lambda r, c: pl.BlockSpec((r, c), lambda b, j: (0, 0))
    out = jax.ShapeDtypeStruct((B, S, W), F32)
    return pl.pallas_call(
        _rwkv_prep_kernel,
        grid=(B, S // ts),
        in_specs=[pl.BlockSpec((1, ts, Wc), lambda b, j: (b, j, 0)), vec(Wc), vec(W), mat(LANES, W), vec(W),
                  mat(LANES, W), mat(LANES, W), vec(W), vec(W), vec(W), mat(W, W)],
        out_specs=[pl.BlockSpec((1, ts, W), lambda b, j: (b, j, 0))] * 8,
        out_shape=[out] * 8,
        scratch_shapes=[pltpu.VMEM((8, Wc), F32)],
        compiler_params=_params(("parallel", "arbitrary"), V7X_VMEM_LIMIT),
        name="rwkv_prep",
    )(cols, mu, w0, wup, a0, aup, gup, k_k, k_a, r_k, hsum)


def _rwkv_scan_kernel(r_ref, lw_ref, k_ref, v_ref, a_ref, b_ref, g_ref, bonus_ref, lng_ref, lnb_ref, hmean_ref,
                      o_ref, state_ref, y_ref):
    C = RWKV_CHUNK
    N = RWKV_HEAD_DIM
    TS = r_ref.shape[1]

    @pl.when(pl.program_id(1) == 0)
    def _():
        state_ref[...] = jnp.zeros_like(state_ref)

    row = lax.broadcasted_iota(I32, (C, C), 0)
    col = lax.broadcasted_iota(I32, (C, C), 1)
    incl = row >= col
    strict = row > col
    eye = (row == col).astype(F32)
    tri = incl.astype(F32)

    def chunk(c, carry):
        sl = pl.ds(pl.multiple_of(c * C, C), C)
        lw = lw_ref[0, sl, :]
        cum = _dot(tri, lw, precision=HI)
        cum_last = cum[C - 1:C, :]
        e_pos = jnp.exp(cum)
        e_neg = jnp.exp(-cum)
        e_last = jnp.exp(cum_last - cum)
        rt = r_ref[0, sl, :] * e_pos
        at = a_ref[0, sl, :] * jnp.exp(cum - lw)
        b = b_ref[0, sl, :]
        k = k_ref[0, sl, :]
        v = v_ref[0, sl, :]
        bt = b * e_neg
        kt = k * e_neg
        bl = b * e_last
        kl = k * e_last
        pc = jnp.exp(cum_last)
        for h in range(RWKV_HEADS):
            s = slice(h * N, (h + 1) * N)
            at_h = at[:, s].astype(BF16)
            rt_h = rt[:, s]
            v_h = v[:, s].astype(BF16)
            lhs2 = jnp.concatenate([at_h, rt_h.astype(BF16)], axis=0)
            sb = _dot(lhs2, bt[:, s].astype(BF16), _NT)
            sk = _dot(lhs2, kt[:, s].astype(BF16), _NT)
            a_ab = jnp.where(strict, sb[:C], 0.0)
            m_rb = jnp.where(incl, sb[C:], 0.0).astype(BF16)
            a_ak = jnp.where(strict, sk[:C], 0.0).astype(BF16)
            m_rk = jnp.where(incl, sk[C:], 0.0).astype(BF16)
            x = a_ab
            t_inv = eye + x
            for _ in range(int(np.log2(C)) - 1):
                xb = x.astype(BF16)
                x = _dot(xb, xb)
                t_inv = t_inv + _dot(t_inv.astype(BF16), x.astype(BF16))
            t_b = t_inv.astype(BF16)
            abar = _dot(t_b, at_h).astype(BF16)
            u0 = _dot(t_b, _dot(a_ak, v_h).astype(BF16)).astype(BF16)
            rbar = rt_h + _dot(m_rb, abar)
            y0 = _dot(m_rb, u0) + _dot(m_rk, v_h)
            bl_h = bl[:, s].astype(BF16)
            gmat = eye * pc[:, s] + _dot(bl_h, abar, _TN)
            hadd = _dot(bl_h, u0, _TN) + _dot(kl[:, s].astype(BF16), v_h, _TN)
            h0 = state_ref[h]
            h0b = h0.astype(BF16)
            y_ref[:, s] = _dot(rbar.astype(BF16), h0b) + y0
            state_ref[h] = _dot(gmat.astype(BF16), h0b) + hadd
        y = y_ref[...]
        mu = _dot(y, hmean_ref[...], precision=HI)
        d = y - mu
        var = _dot(d * d, hmean_ref[...], precision=HI)
        yn = d * lax.rsqrt(var + RWKV_GN_EPS) * lng_ref[...] + lnb_ref[...]
        o_ref[0, sl, :] = ((yn + bonus_ref[0, sl, :]) * g_ref[0, sl, :]).astype(o_ref.dtype)
        return carry

    lax.fori_loop(0, TS // C, chunk, 0)


def _rwkv_scan(r, lw, k, v, a, b, g, bonus, ln_g, ln_b, hmean, ts):
    B, S, W = r.shape
    blk = pl.BlockSpec((1, ts, W), lambda bb, j: (bb, j, 0))
    vec = pl.BlockSpec((1, W), lambda bb, j: (0, 0))
    return pl.pallas_call(
        _rwkv_scan_kernel,
        grid=(B, S // ts),
        in_specs=[blk] * 8 + [vec, vec, pl.BlockSpec((W, W), lambda bb, j: (0, 0))],
        out_specs=blk,
        out_shape=jax.ShapeDtypeStruct((B, S, W), BF16),
        scratch_shapes=[pltpu.VMEM((RWKV_HEADS, RWKV_HEAD_DIM, RWKV_HEAD_DIM), F32),
                        pltpu.VMEM((RWKV_CHUNK, W), F32)],
        compiler_params=_params(("parallel", "arbitrary"), V7X_VMEM_LIMIT),
        name="rwkv_scan",
    )(r, lw, k, v, a, b, g, bonus, ln_g, ln_b, hmean)


def _layer_norm_rows(z, g, b):
    mu = jnp.mean(z, axis=1, keepdims=True)
    d = z - mu
    var = jnp.mean(d * d, axis=1, keepdims=True)
    return d * lax.rsqrt(var + LN_EPS) * g + b


def _pack_bf16_pairs(hn):
    half = hn.shape[1] // 2
    bits = pltpu.bitcast(hn.astype(BF16).astype(F32), U32)
    return (bits[:, :half] >> 16) | (bits[:, half:] & jnp.uint32(0xFFFF0000))


def _outproj_kernel(ya_ref, yb_ref, yc_ref, yd_ref, h_ref, w_ref, g_ref, b_ref, rw_ref, rb_ref,
                    h1_ref, hp_ref, ti_ref, tg_ref):
    mix = _dot(ya_ref[...], w_ref[0]) + _dot(yb_ref[...], w_ref[1]) + _dot(yc_ref[...], w_ref[2]) + _dot(yd_ref[...], w_ref[3])
    hn = _layer_norm_rows(DEEPNORM_ALPHA * h_ref[...] + mix, g_ref[...], b_ref[...])
    h1_ref[...] = hn
    hp_ref[...] = _pack_bf16_pairs(hn)
    tm = hn.shape[0]
    lane = lax.broadcasted_iota(I32, (tm, LANES), 1)
    lane_f = lane.astype(F32)
    logits = _dot(hn, rw_ref[...], precision=HI) + rb_ref[...]
    logits = jnp.where(lane < N_EXPERTS, logits, -jnp.inf)
    top = _topk_lanes(logits, lane_f, TOP_K)
    v0 = top[0][0]
    es = [jnp.exp(vk - v0) for vk, _ in top]
    denom = es[0] + es[1] + es[2] + es[3]
    ti = jnp.zeros((tm, LANES), F32)
    tg = jnp.zeros((tm, LANES), F32)
    for kk in range(TOP_K):
        ti = jnp.where(lane == kk, top[kk][1], ti)
        tg = jnp.where(lane == kk, es[kk] / denom, tg)
    ti_ref[...] = ti.astype(I32)
    tg_ref[...] = tg


def _outproj_ln_router(ys, h, w_out4, ln_g, ln_b, rw, rb, tm):
    T, D = h.shape
    Wg = ys[0].shape[1]
    yspec = pl.BlockSpec((tm, Wg), lambda i: (i, 0))
    row = pl.BlockSpec((tm, D), lambda i: (i, 0))
    vec = lambda n: pl.BlockSpec((1, n), lambda i: (0, 0))
    small = pl.BlockSpec((tm, LANES), lambda i: (i, 0))
    return pl.pallas_call(
        _outproj_kernel,
        grid=(T // tm,),
        in_specs=[yspec] * 4 + [row, pl.BlockSpec((4, Wg, D), lambda i: (0, 0, 0)), vec(D), vec(D),
                                pl.BlockSpec((D, LANES), lambda i: (0, 0)), vec(LANES)],
        out_specs=[row, pl.BlockSpec((tm, D // 2), lambda i: (i, 0)), small, small],
        out_shape=[jax.ShapeDtypeStruct((T, D), F32), jax.ShapeDtypeStruct((T, D // 2), U32),
                   jax.ShapeDtypeStruct((T, LANES), I32), jax.ShapeDtypeStruct((T, LANES), F32)],
        compiler_params=_params(("parallel",), V7X_VMEM_LIMIT),
        name="outproj_ln_router",
    )(*ys, h, w_out4, ln_g, ln_b, rw, rb)


def _row_gather(idx_ref, n, src_hbm, dst_ref, sem):
    def issue(r, carry):
        t = idx_ref[r]
        pltpu.make_async_copy(src_hbm.at[pl.ds(t, 1), :], dst_ref.at[pl.ds(r, 1), :], sem).start()
        return carry

    lax.fori_loop(0, n, issue, 0)

    def drain(r, carry):
        pltpu.make_async_copy(src_hbm.at[pl.ds(0, 1), :], dst_ref.at[pl.ds(r, 1), :], sem).wait()
        return carry

    lax.fori_loop(0, n, drain, 0)


def _moe_kernel(te_ref, nu_ref, src_ref, hp_hbm, w1g_ref, w1l_ref, b1g_ref, b1l_ref, w2_ref, b2_ref,
                o_ref, xu_ref, xb_ref, sem):
    i = pl.program_id(0)
    j = pl.program_id(1)
    nj = pl.num_programs(1)
    tm = xu_ref.shape[0]
    half = xu_ref.shape[1]
    valid = i < nu_ref[0]

    @pl.when(valid & (j == 0))
    def _():
        _row_gather(src_ref.at[0, 0], tm, hp_hbm, xu_ref, sem)
        w = xu_ref[...]
        xb_ref[:, :half] = pltpu.bitcast(w << 16, F32).astype(BF16)
        xb_ref[:, half:] = pltpu.bitcast(w & jnp.uint32(0xFFFF0000), F32).astype(BF16)

    @pl.when(valid)
    def _():
        xb = xb_ref[...]
        ug = _dot(xb, w1g_ref[0].astype(BF16)) + b1g_ref[0]
        ul = _dot(xb, w1l_ref[0].astype(BF16)) + b1l_ref[0]
        glu = jnp.minimum(ug, SWIGLU_LIMIT)
        lin = jnp.clip(ul, -SWIGLU_LIMIT, SWIGLU_LIMIT)
        act = glu * _sigmoid(SWIGLU_ALPHA * glu) * (lin + 1.0)
        contrib = _dot(act.astype(BF16), w2_ref[0].astype(BF16))

        @pl.when(j == 0)
        def _():
            o_ref[...] = contrib + b2_ref[0]

        @pl.when(j > 0)
        def _():
            o_ref[...] += contrib

    @pl.when(jnp.logical_not(valid) & (j == nj - 1))
    def _():
        o_ref[...] = jnp.zeros_like(o_ref)


def _moe_ffn(hp, src_tok, tile_expert, n_used, w1, b1, w2, b2, tm, th):
    NT = src_tok.shape[0]
    E, D, _ = w1.shape
    nH = D_EXPERT // th

    def wj(i, j, te, nu):
        return jnp.where(i < nu[0], j, nH - 1)

    return pl.pallas_call(
        _moe_kernel,
        grid_spec=pltpu.PrefetchScalarGridSpec(
            num_scalar_prefetch=2,
            grid=(NT, nH),
            in_specs=[
                pl.BlockSpec((1, 1, tm), lambda i, j, te, nu: (i, 0, 0), memory_space=pltpu.SMEM),
                pl.BlockSpec(memory_space=pl.ANY),
                pl.BlockSpec((1, D, th), lambda i, j, te, nu: (te[i], 0, wj(i, j, te, nu))),
                pl.BlockSpec((1, D, th), lambda i, j, te, nu: (te[i], 0, nH + wj(i, j, te, nu))),
                pl.BlockSpec((1, 1, th), lambda i, j, te, nu: (te[i], 0, wj(i, j, te, nu))),
                pl.BlockSpec((1, 1, th), lambda i, j, te, nu: (te[i], 0, nH + wj(i, j, te, nu))),
                pl.BlockSpec((1, th, D), lambda i, j, te, nu: (te[i], wj(i, j, te, nu), 0)),
                pl.BlockSpec((1, 1, D), lambda i, j, te, nu: (te[i], 0, 0)),
            ],
            out_specs=pl.BlockSpec((tm, D), lambda i, j, te, nu: (i, 0)),
            scratch_shapes=[pltpu.VMEM((tm, D // 2), U32), pltpu.VMEM((tm, D), BF16), pltpu.SemaphoreType.DMA],
        ),
        out_shape=jax.ShapeDtypeStruct((NT * tm, D), F32),
        compiler_params=_params(("arbitrary", "arbitrary"), V7X_VMEM_LIMIT),
        name="moe_ffn",
    )(tile_expert, n_used, src_tok, hp, w1, w1, b1, b1, w2, b2)


def _combine_kernel(pos_ref, y_hbm, tg_ref, h_ref, g_ref, b_ref, o_ref, buf_ref, sem):
    tq = h_ref.shape[0]
    for kk in range(TOP_K):
        _row_gather(pos_ref.at[0, kk], tq, y_hbm, buf_ref.at[kk], sem)
    tg = tg_ref[...]
    ffn = tg[:, 0:1] * buf_ref[0]
    for kk in range(1, TOP_K):
        ffn = ffn + tg[:, kk:kk + 1] * buf_ref[kk]
    o_ref[...] = _layer_norm_rows(DEEPNORM_ALPHA * h_ref[...] + ffn, g_ref[...], b_ref[...])


def _combine_ln(pos, y, tg, h1, ln_g, ln_b, tq):
    T, D = h1.shape
    row = pl.BlockSpec((tq, D), lambda i: (i, 0))
    vec = pl.BlockSpec((1, D), lambda i: (0, 0))
    return pl.pallas_call(
        _combine_kernel,
        grid=(T // tq,),
        in_specs=[pl.BlockSpec((1, TOP_K, tq), lambda i: (i, 0, 0), memory_space=pltpu.SMEM),
                  pl.BlockSpec(memory_space=pl.ANY), pl.BlockSpec((tq, LANES), lambda i: (i, 0)), row, vec, vec],
        out_specs=row,
        out_shape=jax.ShapeDtypeStruct((T, D), F32),
        scratch_shapes=[pltpu.VMEM((TOP_K, tq, D), F32), pltpu.SemaphoreType.DMA],
        compiler_params=_params(("arbitrary",), V7X_VMEM_LIMIT),
        name="combine_ln",
    )(pos, y, tg, h1, ln_g, ln_b)


def _dispatch_tables(top_idx, tm, n_tiles):
    T = top_idx.shape[0]
    P = T * TOP_K
    e = top_idx.reshape(P)
    onehot = (e[:, None] == jnp.arange(N_EXPERTS, dtype=I32)[None, :]).astype(I32)
    csum = jnp.cumsum(onehot, axis=0)
    rank = jnp.take_along_axis(csum, e[:, None], axis=1)[:, 0] - 1
    counts = csum[-1]
    tiles = (counts + tm - 1) // tm
    tile_end = jnp.cumsum(tiles)
    tile_start = tile_end - tiles
    n_used = tile_end[-1]
    pos = tile_start[e] * tm + rank
    order = jnp.argsort(pos).astype(I32)
    cnt_start = jnp.cumsum(counts) - counts
    tile_ids = jnp.arange(n_tiles, dtype=I32)
    te = jnp.minimum(jnp.searchsorted(tile_end, tile_ids, side="right").astype(I32), N_EXPERTS - 1)
    te = jnp.where(tile_ids < n_used, te, te[jnp.maximum(n_used - 1, 0)])
    rows = jnp.arange(n_tiles * tm, dtype=I32)
    re = te[rows // tm]
    idx_in = rows - tile_start[re] * tm
    ok = (idx_in < counts[re]) & (rows // tm < n_used)
    q = jnp.clip(cnt_start[re] + idx_in, 0, P - 1)
    src_tok = jnp.where(ok, order[q] // TOP_K, 0).astype(I32)
    return pos.astype(I32), src_tok, te, n_used.reshape(1).astype(I32)


def _pad_cols(a, n):
    return jnp.pad(a, ((0, 0), (0, n - a.shape[1])))


def _pad_rows(a, n):
    return jnp.pad(a, ((0, n - a.shape[0]), (0, 0)))


def _rwkv_col_layout(a):
    W = RWKV_W
    rkv, wd, ad, gd = a[:, :3 * W], a[:, 3 * W:3 * W + 64], a[:, 3 * W + 64:3 * W + 128], a[:, 3 * W + 128:]
    return jnp.concatenate([rkv, _pad_cols(wd, LANES), _pad_cols(ad, LANES), gd, jnp.zeros((a.shape[0], LANES), a.dtype)], axis=1)


def _layer(layer, h, B, S, w_in, mu, w0, w_up, a0, a_up, g_up, k_k, k_a, r_k, rln_g, rln_b, fox_b_f, ret_ln_g, ret_ln_b,
           w_out, ln1_g, ln1_b, router_w, router_b, w1, b1, w2, b2, ln2_g, ln2_b):
    T, D = h.shape
    o0 = MOBA_COLS
    o1 = o0 + RWKV_SHIFT_COLS
    o2 = o1 + FOX_COLS
    w_moba = w_in[:, :o0].astype(BF16)
    w_rwkv = _rwkv_col_layout(w_in[:, o0:o1]).astype(BF16)
    w_foxqkv = w_in[:, o1:o1 + 3 * FOX_W].astype(BF16)
    w_foxf = _pad_cols(w_in[:, o1 + 3 * FOX_W:o2], LANES).astype(BF16)
    w_ret = w_in[:, o2:].astype(BF16)

    tm = min(512, T)
    moba_qkv = _matmul(h, w_moba, BF16, tm, 512).reshape(B, S, -1)
    rwkv_cols = _matmul(h, w_rwkv, F32, tm, 512).reshape(B, S, -1)
    fox_qkv = _matmul(h, w_foxqkv, BF16, tm, 512).reshape(B, S, -1)
    fox_f = _matmul(h, w_foxf, F32, tm, LANES).reshape(B, S, -1)
    ret_cols = _matmul(h, w_ret, BF16, tm, 512).reshape(B, S, -1)

    slopes = jnp.exp2(-8.0 * jnp.arange(1, MOBA_HEADS + 1, dtype=F32) / MOBA_HEADS)
    y_a = _moba(moba_qkv, slopes)

    head_of = np.arange(RWKV_W) // RWKV_HEAD_DIM
    same = (head_of[:, None] == head_of[None, :]).astype(np.float32)
    hsum = jnp.asarray(same)
    hmean = jnp.asarray(same / RWKV_HEAD_DIM)
    row = lambda a: a.reshape(1, -1)
    ts_r = min(256, S)
    prep = _rwkv_prep(rwkv_cols, _rwkv_col_layout(row(mu)), row(w0), _pad_rows(w_up, LANES), row(a0),
                      _pad_rows(a_up, LANES), g_up, row(k_k), row(k_a), row(r_k), hsum, ts_r)
    y_b = _rwkv_scan(*prep, row(rln_g), row(rln_b), hmean, min(512, S))

    c = _fox_prep(fox_f, _pad_cols(row(fox_b_f), LANES), min(512, S))
    y_c = _fox(fox_qkv, c, min(256, S))

    y_d = _retention(ret_cols, row(ret_ln_g), row(ret_ln_b), min(256, S))

    Wg = D // 4
    ys = [y.reshape(T, Wg) for y in (y_a, y_b, y_c, y_d)]
    rw = _pad_cols(router_w, LANES)
    rb = _pad_cols(row(router_b), LANES)
    h1, hp, top_i, top_g = _outproj_ln_router(ys, h, w_out.astype(BF16).reshape(4, Wg, D), row(ln1_g), row(ln1_b),
                                              rw, rb, min(256, T))

    tm_e = min(512, T)
    n_tiles = (T * TOP_K) // tm_e + N_EXPERTS
    pos, src_tok, te, n_used = _dispatch_tables(top_i[:, :TOP_K], tm_e, n_tiles)
    y = _moe_ffn(hp, src_tok.reshape(n_tiles, 1, tm_e), te + layer * N_EXPERTS, n_used, w1, b1, w2, b2, tm_e, 512)
    tq = min(128, T)
    pos_t = pos.reshape(T // tq, tq, TOP_K).transpose(0, 2, 1)
    return _combine_ln(pos_t, y, top_g, h1, row(ln2_g), row(ln2_b), tq)


def kernel(x, w_in, rwkv_mu, rwkv_w0, rwkv_w_up, rwkv_a0, rwkv_a_up, rwkv_g_up, rwkv_k_k, rwkv_k_a, rwkv_r_k, rwkv_ln_g, rwkv_ln_b, fox_b_f, ret_ln_g, ret_ln_b, w_out, ln1_g, ln1_b, router_w, router_b, exp_w1, exp_b1, exp_w2, exp_b2, ln2_g, ln2_b):
    B, S, D = x.shape
    h = x.reshape(B * S, D)
    LE = exp_w1.shape[0] * exp_w1.shape[1]
    w1 = exp_w1.reshape(LE, D, -1)
    b1 = exp_b1.reshape(LE, 1, -1)
    w2 = exp_w2.reshape(LE, -1, D)
    b2 = exp_b2.reshape(LE, 1, D)
    for l in range(DEPTH):
        h = _layer(l, h, B, S, w_in[l], rwkv_mu[l], rwkv_w0[l], rwkv_w_up[l], rwkv_a0[l], rwkv_a_up[l], rwkv_g_up[l],
                   rwkv_k_k[l], rwkv_k_a[l], rwkv_r_k[l].reshape(-1), rwkv_ln_g[l], rwkv_ln_b[l], fox_b_f[l],
                   ret_ln_g[l], ret_ln_b[l], w_out[l], ln1_g[l], ln1_b[l], router_w[l], router_b[l],
                   w1, b1, w2, b2, ln2_g[l], ln2_b[l])
    return h.reshape(B, S, D)
```

```python
import functools

import numpy as np
import jax
import jax.numpy as jnp
from jax import lax
from jax.experimental import pallas as pl
from jax.experimental.pallas import tpu as pltpu

F32 = jnp.float32
BF16 = jnp.bfloat16
I32 = jnp.int32
U32 = jnp.uint32
HI = lax.Precision.HIGHEST

D_MODEL = 2048
DEPTH = 2
MOBA_HEADS = 4
MOBA_HEAD_DIM = 128
MOBA_W = MOBA_HEADS * MOBA_HEAD_DIM
MOBA_BLOCK = 256
MOBA_TOPK = 3
RWKV_HEADS = 8
RWKV_HEAD_DIM = 64
RWKV_W = RWKV_HEADS * RWKV_HEAD_DIM
RWKV_W_LORA = 64
RWKV_A_LORA = 64
RWKV_G_LORA = 128
RWKV_GN_EPS = 64e-5
RWKV_CHUNK = 64
FOX_HEADS = 4
FOX_HEAD_DIM = 128
FOX_W = FOX_HEADS * FOX_HEAD_DIM
RET_HEADS = 4
RET_KEY_DIM = 64
RET_VALUE_DIM = 128
RET_W = RET_HEADS * RET_VALUE_DIM
RET_GN_EPS = 1e-5
MOBA_COLS = 3 * MOBA_W
RWKV_SHIFT_COLS = 3 * RWKV_W + RWKV_W_LORA + RWKV_A_LORA + RWKV_G_LORA
FOX_COLS = 3 * FOX_W + FOX_HEADS
RET_COLS = 2 * RET_HEADS * RET_KEY_DIM + 2 * RET_W
N_EXPERTS = 32
TOP_K = 4
D_EXPERT = D_MODEL
SWIGLU_ALPHA = 1.702
SWIGLU_LIMIT = 7.0
LN_EPS = 1e-5
DEEPNORM_ALPHA = (2 * DEPTH) ** 0.25

LANES = 128
V7X_VMEM_LIMIT = 56 * 1024 * 1024
V7X_VMEM_LIMIT_MOE = 60 * 1024 * 1024
MOE_TILE_ROWS = 1024
MOE_STEPS = 4

_NT = (((1,), (1,)), ((), ()))
_TN = (((0,), (0,)), ((), ()))


def _dot(a, b, dims=None, precision=None):
    if dims is None:
        dims = (((a.ndim - 1,), (0,)), ((), ()))
    return lax.dot_general(a, b, dims, precision=precision, preferred_element_type=F32)


def _sigmoid(x):
    return 1.0 / (1.0 + jnp.exp(-x))


def _softplus(x):
    return jnp.maximum(x, 0.0) + jnp.log(1.0 + jnp.exp(-jnp.abs(x)))


def _params(sem, vmem=None):
    return pltpu.CompilerParams(dimension_semantics=sem, vmem_limit_bytes=vmem)


def _mm_kernel(x_ref, w_ref, o_ref, xb_ref):
    @pl.when(pl.program_id(1) == 0)
    def _():
        xb_ref[...] = x_ref[...].astype(BF16)

    o_ref[...] = _dot(xb_ref[...], w_ref[...]).astype(o_ref.dtype)


def _matmul(x, w, out_dtype, tm, tn):
    T, K = x.shape
    N = w.shape[1]
    assert T % tm == 0 and N % tn == 0
    return pl.pallas_call(
        _mm_kernel,
        grid=(T // tm, N // tn),
        in_specs=[pl.BlockSpec((tm, K), lambda i, j: (i, 0)), pl.BlockSpec((K, tn), lambda i, j: (0, j))],
        out_specs=pl.BlockSpec((tm, tn), lambda i, j: (i, j)),
        out_shape=jax.ShapeDtypeStruct((T, N), out_dtype),
        scratch_shapes=[pltpu.VMEM((tm, K), BF16)],
        compiler_params=_params(("parallel", "arbitrary"), V7X_VMEM_LIMIT),
        name="in_proj",
    )(x, w)


def _topk_lanes(vals, lane_f, k):
    outs = []
    g = vals
    for _ in range(k):
        m = jnp.max(g, axis=1, keepdims=True)
        cand = jnp.where((g == m) & (g > -jnp.inf), lane_f, float(LANES))
        idx = jnp.min(cand, axis=1, keepdims=True)
        outs.append((m, idx))
        g = jnp.where(lane_f == idx, -jnp.inf, g)
    return outs


def _moba_kernel(slopes_ref, q_ref, k_ref, v_ref, o_ref, kmean_ref):
    L = MOBA_BLOCK
    S = k_ref.shape[1]
    nb = S // L
    h = pl.program_id(1)
    i = pl.program_id(2)
    scale = MOBA_HEAD_DIM ** -0.5

    @pl.when(i == 0)
    def _():
        kmean_ref[...] = jnp.zeros_like(kmean_ref)
        for n in range(nb):
            kb = k_ref[0, n * L:(n + 1) * L, :].astype(F32)
            kmean_ref[n:n + 1, :] = jnp.sum(kb, axis=0, keepdims=True) * (1.0 / L)

    TQ = q_ref.shape[1]
    G = TQ // L
    q = q_ref[0]
    lane = lax.broadcasted_iota(I32, (TQ, LANES), 1)
    lane_f = lane.astype(F32)
    log2_l = int(np.log2(L))
    row_sub = lax.broadcasted_iota(I32, (TQ, 1), 0) >> log2_l
    gate = _dot(q.astype(F32), kmean_ref[...], _NT, precision=HI)
    gate = jnp.where(lane < G * i + row_sub, gate, -jnp.inf)
    sel = jnp.zeros((TQ, LANES), F32)
    for _, idx in _topk_lanes(gate, lane_f, MOBA_TOPK):
        sel = jnp.where(lane_f == idx, 1.0, sel)

    slope = slopes_ref[h]
    r_i = lax.broadcasted_iota(I32, (TQ, TQ), 0)
    c_i = lax.broadcasted_iota(I32, (TQ, TQ), 1)
    bias0 = (-slope) * (r_i - c_i).astype(F32)
    col_sub = c_i >> log2_l

    def col_penalty(pens):
        out = pens[G - 1]
        for g in range(G - 2, -1, -1):
            out = jnp.where(col_sub == g, pens[g], out)
        return out

    def sel_pen(blk):
        picked = jnp.max(jnp.where(lane == blk, sel, 0.0), axis=1, keepdims=True)
        return jnp.where(picked > 0.0, 0.0, -jnp.inf)

    def tile(start, pens):
        s = _dot(q, k_ref[0, pl.ds(start, TQ), :], _NT) * scale + bias0 + col_penalty(pens)
        return s, v_ref[0, pl.ds(start, TQ), :]

    own = pl.multiple_of(i * TQ, TQ)
    s, vblk = tile(own, [jnp.where(row_sub > g, sel_pen(G * i + g), 0.0) for g in range(G)])
    s = jnp.where(c_i <= r_i, s, -jnp.inf)
    m0 = jnp.max(s, axis=1, keepdims=True)
    p = jnp.exp(s - m0)
    l0 = jnp.sum(p, axis=1, keepdims=True)
    acc0 = _dot(p.astype(BF16), vblk)

    def body(n, carry):
        m, l, acc = carry
        start = pl.multiple_of(n * TQ, TQ)
        off = (-slope) * ((i - n) * TQ).astype(F32)
        s, vblk = tile(start, [sel_pen(G * n + g) + off for g in range(G)])
        m_new = jnp.maximum(m, jnp.max(s, axis=1, keepdims=True))
        alpha = jnp.exp(m - m_new)
        p = jnp.exp(s - m_new)
        l = alpha * l + jnp.sum(p, axis=1, keepdims=True)
        acc = alpha * acc + _dot(p.astype(BF16), vblk)
        return m_new, l, acc

    _, l, acc = lax.fori_loop(0, i, body, (m0, l0, acc0))
    o_ref[0] = (acc / l).astype(o_ref.dtype)


def _moba(qkv, slopes, tq, cb0=0):
    B, S, _ = qkv.shape
    H, L, d = MOBA_HEADS, MOBA_BLOCK, MOBA_HEAD_DIM
    assert S % tq == 0 and tq % L == 0 and S // L <= LANES
    return pl.pallas_call(
        _moba_kernel,
        grid_spec=pltpu.PrefetchScalarGridSpec(
            num_scalar_prefetch=1,
            grid=(B, H, S // tq),
            in_specs=[
                pl.BlockSpec((1, tq, d), lambda b, h, i, sl: (b, i, cb0 + h)),
                pl.BlockSpec((1, S, d), lambda b, h, i, sl: (b, 0, cb0 + H + h)),
                pl.BlockSpec((1, S, d), lambda b, h, i, sl: (b, 0, cb0 + 2 * H + h)),
            ],
            out_specs=pl.BlockSpec((1, tq, d), lambda b, h, i, sl: (b, i, h)),
            scratch_shapes=[pltpu.VMEM((LANES, d), F32)],
        ),
        out_shape=jax.ShapeDtypeStruct((B, S, H * d), BF16),
        compiler_params=_params(("parallel", "parallel", "arbitrary")),
        name="moba",
    )(slopes, qkv, qkv, qkv)


def _fox_prep_kernel(f_ref, bf_ref, c_ref, carry_ref):
    ts = f_ref.shape[1]

    @pl.when(pl.program_id(1) == 0)
    def _():
        carry_ref[...] = jnp.zeros_like(carry_ref)

    z = f_ref[0] + bf_ref[...]
    lf = -_softplus(-z)
    tri = (lax.broadcasted_iota(I32, (ts, ts), 0) >= lax.broadcasted_iota(I32, (ts, ts), 1)).astype(F32)
    c = _dot(tri, lf, precision=HI) + carry_ref[0:1, :]
    c_ref[0] = c
    carry_ref[0:1, :] = c[ts - 1:ts, :]


def _fox_prep(f_logit, b_f, ts, cb=0):
    B, S, _ = f_logit.shape
    W = LANES
    return pl.pallas_call(
        _fox_prep_kernel,
        grid=(B, S // ts),
        in_specs=[pl.BlockSpec((1, ts, W), lambda b, j: (b, j, cb)), pl.BlockSpec((1, W), lambda b, j: (0, 0))],
        out_specs=pl.BlockSpec((1, ts, W), lambda b, j: (b, j, 0)),
        out_shape=jax.ShapeDtypeStruct((B, S, W), F32),
        scratch_shapes=[pltpu.VMEM((8, W), F32)],
        compiler_params=_params(("parallel", "arbitrary")),
        name="fox_prep",
    )(f_logit, b_f)


def _fox_kernel(q_ref, k_ref, v_ref, ccol_ref, crow_ref, o_ref):
    tq = q_ref.shape[1]
    i = pl.program_id(2)
    scale = FOX_HEAD_DIM ** -0.5
    q = q_ref[0]
    cq = ccol_ref[0, 0]
    rc = lax.broadcasted_iota(I32, (tq, tq), 0) - lax.broadcasted_iota(I32, (tq, tq), 1)

    def scores(n):
        start = pl.multiple_of(n * tq, tq)
        s = _dot(q, k_ref[0, pl.ds(start, tq), :], _NT) * scale + cq - crow_ref[0, 0, n]
        return s, v_ref[0, pl.ds(start, tq), :]

    s, vblk = scores(i)
    s = jnp.where(rc >= 0, s, -jnp.inf)
    m0 = jnp.max(s, axis=1, keepdims=True)
    p = jnp.exp(s - m0)
    l0 = jnp.sum(p, axis=1, keepdims=True)
    acc0 = _dot(p.astype(BF16), vblk)

    def body(n, carry):
        m, l, acc = carry
        s, vblk = scores(n)
        m_new = jnp.maximum(m, jnp.max(s, axis=1, keepdims=True))
        alpha = jnp.exp(m - m_new)
        p = jnp.exp(s - m_new)
        l = alpha * l + jnp.sum(p, axis=1, keepdims=True)
        acc = alpha * acc + _dot(p.astype(BF16), vblk)
        return m_new, l, acc

    _, l, acc = lax.fori_loop(0, i, body, (m0, l0, acc0))
    o_ref[0] = (acc / l).astype(o_ref.dtype)


def _fox(qkv, c, tq, cb0=0):
    B, S, _ = qkv.shape
    H, d = FOX_HEADS, FOX_HEAD_DIM
    nq = S // tq
    ch = jnp.transpose(c[:, :, :H], (0, 2, 1))
    c_col = ch[..., None]
    c_row = ch.reshape(B, H, nq, 1, tq)
    return pl.pallas_call(
        _fox_kernel,
        grid=(B, H, nq),
        in_specs=[
            pl.BlockSpec((1, tq, d), lambda b, h, i: (b, i, cb0 + h)),
            pl.BlockSpec((1, S, d), lambda b, h, i: (b, 0, cb0 + H + h)),
            pl.BlockSpec((1, S, d), lambda b, h, i: (b, 0, cb0 + 2 * H + h)),
            pl.BlockSpec((1, 1, tq, 1), lambda b, h, i: (b, h, i, 0)),
            pl.BlockSpec((1, 1, nq, 1, tq), lambda b, h, i: (b, h, 0, 0, 0)),
        ],
        out_specs=pl.BlockSpec((1, tq, d), lambda b, h, i: (b, i, h)),
        out_shape=jax.ShapeDtypeStruct((B, S, H * d), BF16),
        compiler_params=_params(("parallel", "parallel", "arbitrary")),
        name="fox",
    )(qkv, qkv, qkv, c_col, c_row)


def _ret_kernel(q_ref, k_ref, v_ref, g_ref, lng_ref, lnb_ref, o_ref, state_ref):
    C = q_ref.shape[1]
    dk, dv = RET_KEY_DIM, RET_VALUE_DIM

    @pl.when(pl.program_id(1) == 0)
    def _():
        state_ref[...] = jnp.zeros_like(state_ref)

    row = lax.broadcasted_iota(I32, (C, C), 0)
    col = lax.broadcasted_iota(I32, (C, C), 1)
    diff = (row - col).astype(F32)
    tcol = lax.broadcasted_iota(I32, (C, 1), 0).astype(F32)
    scale = dk ** -0.5
    for h in range(RET_HEADS):
        log_gamma = float(np.log(1.0 - 2.0 ** (-5.0 - h)))
        intra = jnp.where(diff >= 0, jnp.exp(log_gamma * jnp.maximum(diff, 0.0)), 0.0)
        q_dec = jnp.exp(log_gamma * (tcol + 1.0))
        k_dec = jnp.exp(log_gamma * (C - 1.0 - tcol))
        blk_dec = float(np.exp(log_gamma * C))
        qh = q_ref[0, :, h * dk:(h + 1) * dk]
        kh = (k_ref[0, :, h * dk:(h + 1) * dk].astype(F32) * scale).astype(BF16)
        vh = v_ref[0, :, h * dv:(h + 1) * dv]
        R = state_ref[h]
        inner = _dot(qh, kh, _NT) * intra
        o = _dot(inner.astype(BF16), vh) + _dot(qh, R.astype(BF16)) * q_dec
        state_ref[h] = R * blk_dec + _dot((kh.astype(F32) * k_dec).astype(BF16), vh, _TN)
        mu = jnp.mean(o, axis=1, keepdims=True)
        d = o - mu
        var = jnp.mean(d * d, axis=1, keepdims=True)
        y = d * lax.rsqrt(var + RET_GN_EPS) * lng_ref[:, h * dv:(h + 1) * dv] + lnb_ref[:, h * dv:(h + 1) * dv]
        g = g_ref[0, :, h * dv:(h + 1) * dv].astype(F32)
        o_ref[0, :, h * dv:(h + 1) * dv] = (g * _sigmoid(g) * y).astype(o_ref.dtype)


def _retention(ret, ln_g, ln_b, C, col0=0):
    B, S, _ = ret.shape
    qk = RET_HEADS * RET_KEY_DIM
    assert col0 % RET_W == 0
    qb, vb = col0 // qk, col0 // RET_W
    return pl.pallas_call(
        _ret_kernel,
        grid=(B, S // C),
        in_specs=[
            pl.BlockSpec((1, C, qk), lambda b, c: (b, c, qb)),
            pl.BlockSpec((1, C, qk), lambda b, c: (b, c, qb + 1)),
            pl.BlockSpec((1, C, RET_W), lambda b, c: (b, c, vb + 1)),
            pl.BlockSpec((1, C, RET_W), lambda b, c: (b, c, vb + 2)),
            pl.BlockSpec((1, RET_W), lambda b, c: (0, 0)),
            pl.BlockSpec((1, RET_W), lambda b, c: (0, 0)),
        ],
        out_specs=pl.BlockSpec((1, C, RET_W), lambda b, c: (b, c, 0)),
        out_shape=jax.ShapeDtypeStruct((B, S, RET_W), BF16),
        scratch_shapes=[pltpu.VMEM((RET_HEADS, RET_KEY_DIM, RET_VALUE_DIM), F32)],
        compiler_params=_params(("parallel", "arbitrary")),
        name="retention",
    )(ret, ret, ret, ret, ln_g, ln_b)


def _rwkv_prep_kernel(x_ref, mu_ref, w0_ref, wup_ref, a0_ref, aup_ref, gup_ref, kk_ref, ka_ref, rk_ref, hsum_ref,
                      r_o, lw_o, k_o, v_o, a_o, b_o, g_o, bonus_o, carry_ref):
    ts = x_ref.shape[1]
    W = RWKV_W

    @pl.when(pl.program_id(1) == 0)
    def _():
        carry_ref[...] = jnp.zeros_like(carry_ref)

    x = x_ref[0]
    row = lax.broadcasted_iota(I32, x.shape, 0)
    shifted = jnp.where(row == 0, carry_ref[0:1, :], pltpu.roll(x, 1, 0))
    carry_ref[0:1, :] = x[ts - 1:ts, :]
    xs = x + (shifted - x) * mu_ref[...]
    r = xs[:, 0:W]
    k = xs[:, W:2 * W]
    v = xs[:, 2 * W:3 * W]
    wd = xs[:, 3 * W:3 * W + LANES]
    ad = xs[:, 3 * W + LANES:3 * W + 2 * LANES]
    gd = xs[:, 3 * W + 2 * LANES:3 * W + 3 * LANES]
    w_log = -_softplus(-(w0_ref[...] + _dot(jnp.tanh(wd), wup_ref[...], precision=HI))) - 0.5
    a = _sigmoid(a0_ref[...] + _dot(ad, aup_ref[...], precision=HI))
    g = _dot(_sigmoid(gd), gup_ref[...], precision=HI)
    kk = k * kk_ref[...]
    norm = jnp.sqrt(_dot(kk * kk, hsum_ref[...], precision=HI))
    kk = kk / jnp.maximum(norm, 1e-12)
    k2 = k * (1.0 + (a - 1.0) * ka_ref[...])
    r_o[0] = r
    lw_o[0] = -jnp.exp(w_log)
    k_o[0] = k2
    v_o[0] = v
    a_o[0] = -kk
    b_o[0] = kk * a
    g_o[0] = g
    bonus_o[0] = _dot(r * k2 * rk_ref[...], hsum_ref[...], precision=HI) * v


def _rwkv_prep(cols, mu, w0, wup, a0, aup, gup, k_k, k_a, r_k, hsum, ts):
    B, S, Wc = cols.shape
    W = RWKV_W
    vec = lambda n: pl.BlockSpec((1, n), lambda b, j: (0, 0))
    mat = lambda r, c: pl.BlockSpec((r, c), lambda b, j: (0, 0))
    out = jax.ShapeDtypeStruct((B, S, W), F32)
    return pl.pallas_call(
        _rwkv_prep_kernel,
        grid=(B, S // ts),
        in_specs=[pl.BlockSpec((1, ts, Wc), lambda b, j: (b, j, 0)), vec(Wc), vec(W), mat(LANES, W), vec(W),
                  mat(LANES, W), mat(LANES, W), vec(W), vec(W), vec(W), mat(W, W)],
        out_specs=[pl.BlockSpec((1, ts, W), lambda b, j: (b, j, 0))] * 8,
        out_shape=[out] * 8,
        scratch_shapes=[pltpu.VMEM((8, Wc), F32)],
        compiler_params=_params(("parallel", "arbitrary"), V7X_VMEM_LIMIT),
        name="rwkv_prep",
    )(cols, mu, w0, wup, a0, aup, gup, k_k, k_a, r_k, hsum)


def _rwkv_scan_kernel(r_ref, lw_ref, k_ref, v_ref, a_ref, b_ref, g_ref, bonus_ref, lng_ref, lnb_ref, hmean_ref,
                      o_ref, state_ref, y_ref):
    C = RWKV_CHUNK
    N = RWKV_HEAD_DIM
    TS = r_ref.shape[1]

    @pl.when(pl.program_id(1) == 0)
    def _():
        state_ref[...] = jnp.zeros_like(state_ref)

    row = lax.broadcasted_iota(I32, (C, C), 0)
    col = lax.broadcasted_iota(I32, (C, C), 1)
    incl = row >= col
    strict = row > col
    eye = (row == col).astype(F32)
    tri = incl.astype(F32)

    def chunk(c, carry):
        sl = pl.ds(pl.multiple_of(c * C, C), C)
        lw = lw_ref[0, sl, :]
        cum = _dot(tri, lw, precision=HI)
        cum_last = cum[C - 1:C, :]
        e_pos = jnp.exp(cum)
        e_neg = jnp.exp(-cum)
        e_last = jnp.exp(cum_last - cum)
        rt = r_ref[0, sl, :] * e_pos
        at = a_ref[0, sl, :] * jnp.exp(cum - lw)
        b = b_ref[0, sl, :]
        k = k_ref[0, sl, :]
        v = v_ref[0, sl, :]
        bt = b * e_neg
        kt = k * e_neg
        bl = b * e_last
        kl = k * e_last
        pc = jnp.exp(cum_last)
        heads = range(RWKV_HEADS)
        sl_h = [slice(h * N, (h + 1) * N) for h in heads]
        at_h = [at[:, s].astype(BF16) for s in sl_h]
        rt_h = [rt[:, s] for s in sl_h]
        v_h = [v[:, s].astype(BF16) for s in sl_h]
        bt_h = [bt[:, s].astype(BF16) for s in sl_h]
        kt_h = [kt[:, s].astype(BF16) for s in sl_h]
        bl_h = [bl[:, s].astype(BF16) for s in sl_h]
        kl_h = [kl[:, s].astype(BF16) for s in sl_h]
        lhs2 = [jnp.concatenate([at_h[h], rt_h[h].astype(BF16)], axis=0) for h in heads]
        sb = [_dot(lhs2[h], bt_h[h], _NT) for h in heads]
        sk = [_dot(lhs2[h], kt_h[h], _NT) for h in heads]
        a_ab = [jnp.where(strict, sb[h][:C], 0.0) for h in heads]
        m_rb = [jnp.where(incl, sb[h][C:], 0.0).astype(BF16) for h in heads]
        a_ak = [jnp.where(strict, sk[h][:C], 0.0).astype(BF16) for h in heads]
        m_rk = [jnp.where(incl, sk[h][C:], 0.0).astype(BF16) for h in heads]
        xb = [a_ab[h].astype(BF16) for h in heads]
        t_inv = [eye + a_ab[h] for h in heads]
        akv = [_dot(a_ak[h], v_h[h]).astype(BF16) for h in heads]
        mrkv = [_dot(m_rk[h], v_h[h]) for h in heads]
        klv = [_dot(kl_h[h], v_h[h], _TN) for h in heads]
        x = [_dot(xb[h], xb[h]) for h in heads]
        for step in range(int(np.log2(C)) - 1):
            xb = [x[h].astype(BF16) for h in heads]
            if step < int(np.log2(C)) - 2:
                x = [_dot(xb[h], xb[h]) for h in heads]
            t_inv = [t_inv[h] + _dot(t_inv[h].astype(BF16), xb[h]) for h in heads]
        t_b = [t_inv[h].astype(BF16) for h in heads]
        abar = [_dot(t_b[h], at_h[h]).astype(BF16) for h in heads]
        u0 = [_dot(t_b[h], akv[h]).astype(BF16) for h in heads]
        rbar = [(rt_h[h] + _dot(m_rb[h], abar[h])).astype(BF16) for h in heads]
        y0 = [_dot(m_rb[h], u0[h]) + mrkv[h] for h in heads]
        gmat = [(eye * pc[:, sl_h[h]] + _dot(bl_h[h], abar[h], _TN)).astype(BF16) for h in heads]
        hadd = [_dot(bl_h[h], u0[h], _TN) + klv[h] for h in heads]
        h0b = [state_ref[h].astype(BF16) for h in heads]
        for h in heads:
            y_ref[:, sl_h[h]] = _dot(rbar[h], h0b[h]) + y0[h]
        for h in heads:
            state_ref[h] = _dot(gmat[h], h0b[h]) + hadd[h]
        y = y_ref[...]
        mu = _dot(y, hmean_ref[...], precision=HI)
        d = y - mu
        var = _dot(d * d, hmean_ref[...], precision=HI)
        yn = d * lax.rsqrt(var + RWKV_GN_EPS) * lng_ref[...] + lnb_ref[...]
        o_ref[0, sl, :] = ((yn + bonus_ref[0, sl, :]) * g_ref[0, sl, :]).astype(o_ref.dtype)
        return carry

    lax.fori_loop(0, TS // C, chunk, 0)


def _rwkv_scan(r, lw, k, v, a, b, g, bonus, ln_g, ln_b, hmean, ts):
    B, S, W = r.shape
    blk = pl.BlockSpec((1, ts, W), lambda bb, j: (bb, j, 0))
    vec = pl.BlockSpec((1, W), lambda bb, j: (0, 0))
    return pl.pallas_call(
        _rwkv_scan_kernel,
        grid=(B, S // ts),
        in_specs=[blk] * 8 + [vec, vec, pl.BlockSpec((W, W), lambda bb, j: (0, 0))],
        out_specs=blk,
        out_shape=jax.ShapeDtypeStruct((B, S, W), BF16),
        scratch_shapes=[pltpu.VMEM((RWKV_HEADS, RWKV_HEAD_DIM, RWKV_HEAD_DIM), F32),
                        pltpu.VMEM((RWKV_CHUNK, W), F32)],
        compiler_params=_params(("parallel", "arbitrary"), V7X_VMEM_LIMIT),
        name="rwkv_scan",
    )(r, lw, k, v, a, b, g, bonus, ln_g, ln_b, hmean)


def _layer_norm_rows(z, g, b):
    mu = jnp.mean(z, axis=1, keepdims=True)
    d = z - mu
    var = jnp.mean(d * d, axis=1, keepdims=True)
    return d * lax.rsqrt(var + LN_EPS) * g + b


def _pack_bf16_pairs(lo, hi):
    bits = lambda a: pltpu.bitcast(a.astype(BF16).astype(F32), U32)
    return (bits(lo) >> 16) | (bits(hi) & jnp.uint32(0xFFFF0000))


def _outproj_kernel(ya_ref, yb_ref, yc_ref, yd_ref, h_ref, w_ref, g_ref, b_ref, rw_ref, rb_ref,
                    h1_ref, hp_ref, ti_ref, tg_ref):
    mix = _dot(ya_ref[...], w_ref[0]) + _dot(yb_ref[...], w_ref[1]) + _dot(yc_ref[...], w_ref[2]) + _dot(yd_ref[...], w_ref[3])
    hn = _layer_norm_rows(DEEPNORM_ALPHA * h_ref[...] + mix, g_ref[...], b_ref[...])
    h1_ref[...] = hn
    hp_ref[...] = _pack_bf16_pairs(hn[:, :hn.shape[1] // 2], hn[:, hn.shape[1] // 2:])
    tm = hn.shape[0]
    lane = lax.broadcasted_iota(I32, (tm, LANES), 1)
    lane_f = lane.astype(F32)
    hn_hi = hn.astype(BF16)
    hn_lo = (hn - hn_hi.astype(F32)).astype(BF16)
    both = _dot(hn_hi, rw_ref[...])
    logits = both[:, :LANES] + (both[:, LANES:] + _dot(hn_lo, rw_ref[:, :LANES])) + rb_ref[...]
    logits = jnp.where(lane < N_EXPERTS, logits, -jnp.inf)
    top = _topk_lanes(logits, lane_f, TOP_K)
    v0 = top[0][0]
    es = [jnp.exp(vk - v0) for vk, _ in top]
    denom = es[0] + es[1] + es[2] + es[3]
    ti = jnp.zeros((tm, LANES), F32)
    tg = jnp.zeros((tm, LANES), F32)
    for kk in range(TOP_K):
        ti = jnp.where(lane == kk, top[kk][1], ti)
        tg = jnp.where(lane == kk, es[kk] / denom, tg)
    ti_ref[...] = ti.astype(I32)
    tg_ref[...] = tg


def _outproj_ln_router(ys, h, w_out4, ln_g, ln_b, rw, rb, tm):
    T, D = h.shape
    Wg = ys[0].shape[1]
    yspec = pl.BlockSpec((tm, Wg), lambda i: (i, 0))
    row = pl.BlockSpec((tm, D), lambda i: (i, 0))
    vec = lambda n: pl.BlockSpec((1, n), lambda i: (0, 0))
    small = pl.BlockSpec((tm, LANES), lambda i: (i, 0))
    return pl.pallas_call(
        _outproj_kernel,
        grid=(T // tm,),
        in_specs=[yspec] * 4 + [row, pl.BlockSpec((4, Wg, D), lambda i: (0, 0, 0)), vec(D), vec(D),
                                pl.BlockSpec((D, 2 * LANES), lambda i: (0, 0)), vec(LANES)],
        out_specs=[row, pl.BlockSpec((tm, D // 2), lambda i: (i, 0)), small, small],
        out_shape=[jax.ShapeDtypeStruct((T, D), F32), jax.ShapeDtypeStruct((T, D // 2), U32),
                   jax.ShapeDtypeStruct((T, LANES), I32), jax.ShapeDtypeStruct((T, LANES), F32)],
        compiler_params=_params(("parallel",), V7X_VMEM_LIMIT),
        name="outproj_ln_router",
    )(*ys, h, w_out4, ln_g, ln_b, rw, rb)


def _unpack_bf16_pairs(w):
    return pltpu.bitcast(w << 16, F32), pltpu.bitcast(w & jnp.uint32(0xFFFF0000), F32)


def _moe_kernel(te_ref, nu_ref, src0_ref, srcn_ref, dstp_ref, hp_hbm, w1g_ref, w1l_ref, b1g_ref, b1l_ref, w2a_ref,
                w2b_ref, b2a_ref, b2b_ref, y_hbm, xu_ref, xb_ref, act_ref, acc_ref, gsem, ssem, *, n_real_rows):
    i = pl.program_id(0)
    j = pl.program_id(1)
    ns = MOE_STEPS
    nj = 2 * ns
    tm, half = xu_ref.shape
    rows_per_step = tm // nj
    th = act_ref.shape[1] // ns
    tn = half // ns
    nu = nu_ref[0]
    valid = i < nu
    slot = i % 2
    base = j * rows_per_step

    def wait_gather():
        pltpu.make_async_copy(hp_hbm.at[pl.ds(0, tm), :], xu_ref, gsem).wait()

    def wait_scatter():
        pltpu.make_async_copy(acc_ref.at[0], y_hbm.at[pl.ds(0, tm), :], ssem).wait()

    def issue_gather_chunk():
        for r in range(rows_per_step):
            t = srcn_ref[0, 0, base + r]
            pltpu.make_async_copy(hp_hbm.at[pl.ds(t, 1), :], xu_ref.at[pl.ds(base + r, 1), :], gsem).start()

    def issue_scatter_chunk():
        for r in range(rows_per_step):
            t = dstp_ref[0, 0, base + r]
            pltpu.make_async_copy(acc_ref.at[1 - slot, pl.ds(base + r, 1), :], y_hbm.at[pl.ds(t, 1), :], ssem).start()

    @pl.when((i == 0) & (j == 0))
    def _():
        acc_ref[1] = jnp.zeros(acc_ref.shape[1:], U32)
        fills = [pltpu.make_async_copy(acc_ref.at[1], y_hbm.at[pl.ds(n_real_rows + f * tm, tm), :], ssem)
                 for f in range((y_hbm.shape[0] - n_real_rows) // tm)]
        for fill in fills:
            fill.start()
        for fill in fills:
            fill.wait()

        def issue(r, carry):
            t = src0_ref[0, 0, r]
            pltpu.make_async_copy(hp_hbm.at[pl.ds(t, 1), :], xu_ref.at[pl.ds(r, 1), :], gsem).start()
            return carry

        lax.fori_loop(0, tm, issue, 0)

    @pl.when((j == 0) & (i <= nu))
    def _():
        wait_gather()

    @pl.when(valid & (j == 0))
    def _():
        lo, hi = _unpack_bf16_pairs(xu_ref[...])
        xb_ref[:, :half] = lo.astype(BF16)
        xb_ref[:, half:] = hi.astype(BF16)

    @pl.when(valid & (j < ns))
    def _():
        xb = xb_ref[...]
        w1g = w1g_ref[0].astype(BF16)
        w1l = w1l_ref[0].astype(BF16)
        b1g = b1g_ref[0]
        b1l = b1l_ref[0]
        acts = []
        for c in range(2):
            cs = slice(c * (th // 2), (c + 1) * (th // 2))
            glu = jnp.minimum(_dot(xb, w1g[:, cs]) + b1g[:, cs], SWIGLU_LIMIT)
            lin = jnp.clip(_dot(xb, w1l[:, cs]) + b1l[:, cs], -SWIGLU_LIMIT, SWIGLU_LIMIT)
            acts.append((glu * _sigmoid(SWIGLU_ALPHA * glu) * (lin + 1.0)).astype(BF16))
        issue_gather_chunk()
        issue_scatter_chunk()
        act = jnp.concatenate(acts, axis=1)
        for jj in range(ns):
            @pl.when(j == jj)
            def _():
                act_ref[:, jj * th:(jj + 1) * th] = act

    @pl.when(valid & (j >= ns))
    def _():
        act = act_ref[...]
        lo = _dot(act, w2a_ref[0].astype(BF16)) + b2a_ref[0]
        issue_gather_chunk()
        hi = _dot(act, w2b_ref[0].astype(BF16)) + b2b_ref[0]
        issue_scatter_chunk()
        packed = _pack_bf16_pairs(lo, hi)
        for nn in range(ns):
            @pl.when(j == ns + nn)
            def _():
                acc_ref[slot, :, nn * tn:(nn + 1) * tn] = packed

        @pl.when(j == nj - 1)
        def _():
            wait_scatter()

    @pl.when(i == nu)
    def _():
        issue_scatter_chunk()

        @pl.when(j == nj - 1)
        def _():
            wait_scatter()


def _moe_ffn(hp, src_tok, dst_row, tile_expert, n_used, w1, b1, w2, b2, tm):
    NT = src_tok.shape[0]
    _, D, _ = w1.shape
    ns = MOE_STEPS
    nj = 2 * ns
    half = D // 2
    th = D_EXPERT // ns
    tn = half // ns
    assert tm % nj == 0

    def step(i, j, nu):
        return jnp.where(i < nu[0], j, nj - 1)

    def j1(i, j, te, nu):
        return jnp.minimum(step(i, j, nu), ns - 1)

    def j2(i, j, te, nu):
        return jnp.maximum(step(i, j, nu) - ns, 0)

    smem = lambda f: pl.BlockSpec((1, 1, tm), f, memory_space=pltpu.SMEM)
    assert (NT * tm - hp.shape[0] * TOP_K) % tm == 0
    return pl.pallas_call(
        functools.partial(_moe_kernel, n_real_rows=hp.shape[0] * TOP_K),
        grid_spec=pltpu.PrefetchScalarGridSpec(
            num_scalar_prefetch=2,
            grid=(NT, nj),
            in_specs=[
                smem(lambda i, j, te, nu: (0, 0, 0)),
                smem(lambda i, j, te, nu: (jnp.minimum(i + 1, NT - 1), 0, 0)),
                smem(lambda i, j, te, nu: (jnp.maximum(i - 1, 0), 0, 0)),
                pl.BlockSpec(memory_space=pl.ANY),
                pl.BlockSpec((1, D, th), lambda i, j, te, nu: (te[i], 0, j1(i, j, te, nu))),
                pl.BlockSpec((1, D, th), lambda i, j, te, nu: (te[i], 0, ns + j1(i, j, te, nu))),
                pl.BlockSpec((1, 1, th), lambda i, j, te, nu: (te[i], 0, j1(i, j, te, nu))),
                pl.BlockSpec((1, 1, th), lambda i, j, te, nu: (te[i], 0, ns + j1(i, j, te, nu))),
                pl.BlockSpec((1, D_EXPERT, tn), lambda i, j, te, nu: (te[i], 0, j2(i, j, te, nu))),
                pl.BlockSpec((1, D_EXPERT, tn), lambda i, j, te, nu: (te[i], 0, ns + j2(i, j, te, nu))),
                pl.BlockSpec((1, 1, tn), lambda i, j, te, nu: (te[i], 0, j2(i, j, te, nu))),
                pl.BlockSpec((1, 1, tn), lambda i, j, te, nu: (te[i], 0, ns + j2(i, j, te, nu))),
            ],
            out_specs=pl.BlockSpec(memory_space=pl.ANY),
            scratch_shapes=[pltpu.VMEM((tm, half), U32), pltpu.VMEM((tm, D), BF16), pltpu.VMEM((tm, D_EXPERT), BF16),
                            pltpu.VMEM((2, tm, half), U32), pltpu.SemaphoreType.DMA, pltpu.SemaphoreType.DMA],
        ),
        out_shape=jax.ShapeDtypeStruct((NT * tm, half), U32),
        compiler_params=_params(("arbitrary", "arbitrary"), V7X_VMEM_LIMIT_MOE),
        name="moe_ffn",
    )(tile_expert, n_used, src_tok, src_tok, dst_row, hp, w1, w1, b1, b1, w2, w2, b2, b2)


def _combine_kernel(y_ref, tg_ref, h_ref, g_ref, b_ref, o_ref):
    half = h_ref.shape[1] // 2
    tg = tg_ref[...]
    lo = hi = None
    for kk in range(TOP_K):
        ylo, yhi = _unpack_bf16_pairs(y_ref[:, kk * half:(kk + 1) * half])
        gk = tg[:, kk:kk + 1]
        lo = gk * ylo if lo is None else lo + gk * ylo
        hi = gk * yhi if hi is None else hi + gk * yhi
    ffn = jnp.concatenate([lo, hi], axis=1)
    o_ref[...] = _layer_norm_rows(DEEPNORM_ALPHA * h_ref[...] + ffn, g_ref[...], b_ref[...])


def _combine_ln(y4, tg, h1, ln_g, ln_b, tq):
    T, D = h1.shape
    row = pl.BlockSpec((tq, D), lambda i: (i, 0))
    vec = pl.BlockSpec((1, D), lambda i: (0, 0))
    return pl.pallas_call(
        _combine_kernel,
        grid=(T // tq,),
        in_specs=[pl.BlockSpec((tq, TOP_K * D // 2), lambda i: (i, 0)), pl.BlockSpec((tq, LANES), lambda i: (i, 0)), row, vec, vec],
        out_specs=row,
        out_shape=jax.ShapeDtypeStruct((T, D), F32),
        compiler_params=_params(("parallel",), V7X_VMEM_LIMIT),
        name="combine_ln",
    )(y4, tg, h1, ln_g, ln_b)


def _dispatch_tables(top_idx, tm, n_tiles):
    T = top_idx.shape[0]
    P = T * TOP_K
    E = N_EXPERTS
    e = top_idx.reshape(P)
    experts = jnp.arange(E, dtype=I32)
    order = jnp.sort(e * P + jnp.arange(P, dtype=I32)) % P
    counts = jnp.sum((e[:, None] == experts[None, :]).astype(I32), axis=0)
    cnt_start = jnp.cumsum(counts) - counts
    tiles = (counts + tm - 1) // tm
    tile_end = jnp.cumsum(tiles)
    tile_start = tile_end - tiles
    n_used = tile_end[-1]
    tile_ids = jnp.arange(n_tiles, dtype=I32)
    last = jnp.maximum(n_used - 1, 0)
    te = jnp.sum((tile_end[None, :] <= jnp.minimum(tile_ids, last)[:, None]).astype(I32), axis=1)
    onehot = (te[:, None] == experts[None, :]).astype(I32)
    pick = lambda table: jnp.sum(onehot * table[None, :], axis=1)
    first = (tile_ids - pick(tile_start)) * tm
    n_valid = jnp.where(tile_ids < n_used, jnp.clip(pick(counts) - first, 0, tm), 0)
    valid_before = jnp.cumsum(n_valid) - n_valid
    start = jnp.clip(pick(cnt_start) + first, 0, P)
    order_pad = jnp.concatenate([order, jnp.zeros((tm,), I32)])
    pair = jax.vmap(lambda s: lax.dynamic_slice(order_pad, (s,), (tm,)))(start)
    r = jnp.arange(tm, dtype=I32)[None, :]
    ok = r < n_valid[:, None]
    dummy = P + (tile_ids * tm - valid_before - n_valid)[:, None] + r
    src_tok = jnp.where(ok, pair // TOP_K, 0).astype(I32)
    dst_row = jnp.where(ok, pair, dummy).astype(I32)
    return src_tok, dst_row, te.astype(I32), n_used.reshape(1).astype(I32)


def _pad_cols(a, n):
    return jnp.pad(a, ((0, 0), (0, n - a.shape[1])))


def _pad_rows(a, n):
    return jnp.pad(a, ((0, n - a.shape[0]), (0, 0)))


def _rwkv_col_layout(a, tail=None):
    W = RWKV_W
    rkv, wd, ad, gd = a[:, :3 * W], a[:, 3 * W:3 * W + 64], a[:, 3 * W + 64:3 * W + 128], a[:, 3 * W + 128:]
    if tail is None:
        tail = jnp.zeros((a.shape[0], LANES), a.dtype)
    return jnp.concatenate([rkv, _pad_cols(wd, LANES), _pad_cols(ad, LANES), gd, tail], axis=1)


def _layer(layer, h, B, S, w_in, mu, w0, w_up, a0, a_up, g_up, k_k, k_a, r_k, rln_g, rln_b, fox_b_f, ret_ln_g, ret_ln_b,
           w_out, ln1_g, ln1_b, router_w, router_b, w1, b1, w2, b2, ln2_g, ln2_b):
    T, D = h.shape
    o0 = MOBA_COLS
    o1 = o0 + RWKV_SHIFT_COLS
    o2 = o1 + FOX_COLS
    w_att = jnp.concatenate([w_in[:, :o0], w_in[:, o1:o1 + 3 * FOX_W], w_in[:, o2:]], axis=1).astype(BF16)
    w_f32 = _rwkv_col_layout(w_in[:, o0:o1], _pad_cols(w_in[:, o1 + 3 * FOX_W:o2], LANES)).astype(BF16)
    fox_cb0 = MOBA_COLS // FOX_HEAD_DIM
    ret_col0 = MOBA_COLS + 3 * FOX_W
    foxf_cb = w_f32.shape[1] // LANES - 1

    tm = min(512, T)
    att = _matmul(h, w_att, BF16, tm, MOBA_COLS).reshape(B, S, -1)
    rwkv_cols = _matmul(h, w_f32, F32, tm, 1024).reshape(B, S, -1)

    slopes = jnp.exp2(-8.0 * jnp.arange(1, MOBA_HEADS + 1, dtype=F32) / MOBA_HEADS)
    y_a = _moba(att, slopes, min(512, S))

    head_of = np.arange(RWKV_W) // RWKV_HEAD_DIM
    same = (head_of[:, None] == head_of[None, :]).astype(np.float32)
    hsum = jnp.asarray(same)
    hmean = jnp.asarray(same / RWKV_HEAD_DIM)
    row = lambda a: a.reshape(1, -1)
    ts_r = min(256, S)
    prep = _rwkv_prep(rwkv_cols, _rwkv_col_layout(row(mu)), row(w0), _pad_rows(w_up, LANES), row(a0),
                      _pad_rows(a_up, LANES), g_up, row(k_k), row(k_a), row(r_k), hsum, ts_r)
    y_b = _rwkv_scan(*prep, row(rln_g), row(rln_b), hmean, min(512, S))

    c = _fox_prep(rwkv_cols, _pad_cols(row(fox_b_f), LANES), min(512, S), foxf_cb)
    y_c = _fox(att, c, min(512, S), fox_cb0)

    y_d = _retention(att, row(ret_ln_g), row(ret_ln_b), min(256, S), ret_col0)

    Wg = D // 4
    ys = [y.reshape(T, Wg) for y in (y_a, y_b, y_c, y_d)]
    rw32 = _pad_cols(router_w, LANES)
    rw_hi = rw32.astype(BF16)
    rw = jnp.concatenate([rw_hi, (rw32 - rw_hi.astype(F32)).astype(BF16)], axis=1)
    rb = _pad_cols(row(router_b), LANES)
    h1, hp, top_i, top_g = _outproj_ln_router(ys, h, w_out.astype(BF16).reshape(4, Wg, D), row(ln1_g), row(ln1_b),
                                              rw, rb, min(256, T))

    tm_e = min(MOE_TILE_ROWS, T)
    n_tiles = (T * TOP_K) // tm_e + N_EXPERTS + 1
    src_tok, dst_row, te, n_used = _dispatch_tables(top_i[:, :TOP_K], tm_e, n_tiles)
    y = _moe_ffn(hp, src_tok.reshape(n_tiles, 1, tm_e), dst_row.reshape(n_tiles, 1, tm_e), te + layer * N_EXPERTS,
                 n_used, w1, b1, w2, b2, tm_e)
    y4 = y.reshape(-1, TOP_K * D // 2)
    return _combine_ln(y4, top_g, h1, row(ln2_g), row(ln2_b), min(256, T))


def kernel(x, w_in, rwkv_mu, rwkv_w0, rwkv_w_up, rwkv_a0, rwkv_a_up, rwkv_g_up, rwkv_k_k, rwkv_k_a, rwkv_r_k, rwkv_ln_g, rwkv_ln_b, fox_b_f, ret_ln_g, ret_ln_b, w_out, ln1_g, ln1_b, router_w, router_b, exp_w1, exp_b1, exp_w2, exp_b2, ln2_g, ln2_b):
    B, S, D = x.shape
    h = x.reshape(B * S, D)
    LE = exp_w1.shape[0] * exp_w1.shape[1]
    w1 = exp_w1.reshape(LE, D, -1)
    b1 = exp_b1.reshape(LE, 1, -1)
    w2 = exp_w2.reshape(LE, -1, D)
    b2 = exp_b2.reshape(LE, 1, D)
    for l in range(DEPTH):
        h = _layer(l, h, B, S, w_in[l], rwkv_mu[l], rwkv_w0[l], rwkv_w_up[l], rwkv_a0[l], rwkv_a_up[l], rwkv_g_up[l],
                   rwkv_k_k[l], rwkv_k_a[l], rwkv_r_k[l].reshape(-1), rwkv_ln_g[l], rwkv_ln_b[l], fox_b_f[l],
                   ret_ln_g[l], ret_ln_b[l], w_out[l], ln1_g[l], ln1_b[l], router_w[l], router_b[l],
                   w1, b1, w2, b2, ln2_g[l], ln2_b[l])
    return h.reshape(B, S, D)
```

```python
import functools

import numpy as np
import jax
import jax.numpy as jnp
from jax import lax
from jax.experimental import pallas as pl
from jax.experimental.pallas import tpu as pltpu

F32 = jnp.float32
BF16 = jnp.bfloat16
I32 = jnp.int32
U32 = jnp.uint32
HI = lax.Precision.HIGHEST

D_MODEL = 2048
DEPTH = 2
MOBA_HEADS = 4
MOBA_HEAD_DIM = 128
MOBA_W = MOBA_HEADS * MOBA_HEAD_DIM
MOBA_BLOCK = 256
MOBA_TOPK = 3
RWKV_HEADS = 8
RWKV_HEAD_DIM = 64
RWKV_W = RWKV_HEADS * RWKV_HEAD_DIM
RWKV_W_LORA = 64
RWKV_A_LORA = 64
RWKV_G_LORA = 128
RWKV_GN_EPS = 64e-5
RWKV_CHUNK = 64
FOX_HEADS = 4
FOX_HEAD_DIM = 128
FOX_W = FOX_HEADS * FOX_HEAD_DIM
RET_HEADS = 4
RET_KEY_DIM = 64
RET_VALUE_DIM = 128
RET_W = RET_HEADS * RET_VALUE_DIM
RET_GN_EPS = 1e-5
MOBA_COLS = 3 * MOBA_W
RWKV_SHIFT_COLS = 3 * RWKV_W + RWKV_W_LORA + RWKV_A_LORA + RWKV_G_LORA
FOX_COLS = 3 * FOX_W + FOX_HEADS
RET_COLS = 2 * RET_HEADS * RET_KEY_DIM + 2 * RET_W
N_EXPERTS = 32
TOP_K = 4
D_EXPERT = D_MODEL
SWIGLU_ALPHA = 1.702
SWIGLU_LIMIT = 7.0
LN_EPS = 1e-5
DEEPNORM_ALPHA = (2 * DEPTH) ** 0.25

LANES = 128
V7X_VMEM_LIMIT = 56 * 1024 * 1024
V7X_VMEM_LIMIT_MOE = 60 * 1024 * 1024
MOE_TILE_ROWS = 1024
MOE_STEPS = 4

_NT = (((1,), (1,)), ((), ()))
_TN = (((0,), (0,)), ((), ()))


def _dot(a, b, dims=None, precision=None):
    if dims is None:
        dims = (((a.ndim - 1,), (0,)), ((), ()))
    return lax.dot_general(a, b, dims, precision=precision, preferred_element_type=F32)


def _sigmoid(x):
    return 1.0 / (1.0 + jnp.exp(-x))


def _softplus(x):
    return jnp.maximum(x, 0.0) + jnp.log(1.0 + jnp.exp(-jnp.abs(x)))


def _params(sem, vmem=None):
    return pltpu.CompilerParams(dimension_semantics=sem, vmem_limit_bytes=vmem)


def _mm_kernel(x_ref, w_ref, o_ref, xb_ref):
    @pl.when(pl.program_id(1) == 0)
    def _():
        xb_ref[...] = x_ref[...].astype(BF16)

    o_ref[...] = _dot(xb_ref[...], w_ref[...]).astype(o_ref.dtype)


def _matmul(x, w, out_dtype, tm, tn):
    T, K = x.shape
    N = w.shape[1]
    assert T % tm == 0 and N % tn == 0
    return pl.pallas_call(
        _mm_kernel,
        grid=(T // tm, N // tn),
        in_specs=[pl.BlockSpec((tm, K), lambda i, j: (i, 0)), pl.BlockSpec((K, tn), lambda i, j: (0, j))],
        out_specs=pl.BlockSpec((tm, tn), lambda i, j: (i, j)),
        out_shape=jax.ShapeDtypeStruct((T, N), out_dtype),
        scratch_shapes=[pltpu.VMEM((tm, K), BF16)],
        compiler_params=_params(("parallel", "arbitrary"), V7X_VMEM_LIMIT),
        name="in_proj",
    )(x, w)


def _topk_lanes(vals, lane_f, k):
    outs = []
    g = vals
    for _ in range(k):
        m = jnp.max(g, axis=1, keepdims=True)
        cand = jnp.where((g == m) & (g > -jnp.inf), lane_f, float(LANES))
        idx = jnp.min(cand, axis=1, keepdims=True)
        outs.append((m, idx))
        g = jnp.where(lane_f == idx, -jnp.inf, g)
    return outs


def _moba_kernel(slopes_ref, q_ref, k_ref, v_ref, o_ref, kmean_ref):
    L = MOBA_BLOCK
    S = k_ref.shape[1]
    nb = S // L
    h = pl.program_id(1)
    i = pl.program_id(2)
    scale = MOBA_HEAD_DIM ** -0.5

    @pl.when(i == 0)
    def _():
        kmean_ref[...] = jnp.zeros_like(kmean_ref)
        for n in range(nb):
            kb = k_ref[0, n * L:(n + 1) * L, :].astype(F32)
            kmean_ref[n:n + 1, :] = jnp.sum(kb, axis=0, keepdims=True) * (1.0 / L)

    TQ = q_ref.shape[1]
    G = TQ // L
    q = q_ref[0]
    lane = lax.broadcasted_iota(I32, (TQ, LANES), 1)
    lane_f = lane.astype(F32)
    log2_l = int(np.log2(L))
    row_sub = lax.broadcasted_iota(I32, (TQ, 1), 0) >> log2_l
    gate = _dot(q.astype(F32), kmean_ref[...], _NT, precision=HI)
    gate = jnp.where(lane < G * i + row_sub, gate, -jnp.inf)
    sel = jnp.zeros((TQ, LANES), F32)
    for _, idx in _topk_lanes(gate, lane_f, MOBA_TOPK):
        sel = jnp.where(lane_f == idx, 1.0, sel)

    slope = slopes_ref[h]
    r_i = lax.broadcasted_iota(I32, (TQ, TQ), 0)
    c_i = lax.broadcasted_iota(I32, (TQ, TQ), 1)
    bias0 = (-slope) * (r_i - c_i).astype(F32)
    col_sub = c_i >> log2_l

    def col_penalty(pens):
        out = pens[G - 1]
        for g in range(G - 2, -1, -1):
            out = jnp.where(col_sub == g, pens[g], out)
        return out

    def sel_pen(blk):
        picked = jnp.max(jnp.where(lane == blk, sel, 0.0), axis=1, keepdims=True)
        return jnp.where(picked > 0.0, 0.0, -jnp.inf)

    def tile(start, pens):
        s = _dot(q, k_ref[0, pl.ds(start, TQ), :], _NT) * scale + bias0 + col_penalty(pens)
        return s, v_ref[0, pl.ds(start, TQ), :]

    own = pl.multiple_of(i * TQ, TQ)
    s, vblk = tile(own, [jnp.where(row_sub > g, sel_pen(G * i + g), 0.0) for g in range(G)])
    s = jnp.where(c_i <= r_i, s, -jnp.inf)
    m0 = jnp.max(s, axis=1, keepdims=True)
    p = jnp.exp(s - m0)
    l0 = jnp.sum(p, axis=1, keepdims=True)
    acc0 = _dot(p.astype(BF16), vblk)

    def body(n, carry):
        m, l, acc = carry
        start = pl.multiple_of(n * TQ, TQ)
        off = (-slope) * ((i - n) * TQ).astype(F32)
        s, vblk = tile(start, [sel_pen(G * n + g) + off for g in range(G)])
        m_new = jnp.maximum(m, jnp.max(s, axis=1, keepdims=True))
        alpha = jnp.exp(m - m_new)
        p = jnp.exp(s - m_new)
        l = alpha * l + jnp.sum(p, axis=1, keepdims=True)
        acc = alpha * acc + _dot(p.astype(BF16), vblk)
        return m_new, l, acc

    _, l, acc = lax.fori_loop(0, i, body, (m0, l0, acc0))
    o_ref[0] = (acc / l).astype(o_ref.dtype)


def _moba(qkv, slopes, tq, cb0=0):
    B, S, _ = qkv.shape
    H, L, d = MOBA_HEADS, MOBA_BLOCK, MOBA_HEAD_DIM
    assert S % tq == 0 and tq % L == 0 and S // L <= LANES
    return pl.pallas_call(
        _moba_kernel,
        grid_spec=pltpu.PrefetchScalarGridSpec(
            num_scalar_prefetch=1,
            grid=(B, H, S // tq),
            in_specs=[
                pl.BlockSpec((1, tq, d), lambda b, h, i, sl: (b, i, cb0 + h)),
                pl.BlockSpec((1, S, d), lambda b, h, i, sl: (b, 0, cb0 + H + h)),
                pl.BlockSpec((1, S, d), lambda b, h, i, sl: (b, 0, cb0 + 2 * H + h)),
            ],
            out_specs=pl.BlockSpec((1, tq, d), lambda b, h, i, sl: (b, i, h)),
            scratch_shapes=[pltpu.VMEM((LANES, d), F32)],
        ),
        out_shape=jax.ShapeDtypeStruct((B, S, H * d), BF16),
        compiler_params=_params(("parallel", "parallel", "arbitrary")),
        name="moba",
    )(slopes, qkv, qkv, qkv)


def _fox_prep_kernel(f_ref, bf_ref, c_ref, carry_ref):
    ts = f_ref.shape[1]

    @pl.when(pl.program_id(1) == 0)
    def _():
        carry_ref[...] = jnp.zeros_like(carry_ref)

    z = f_ref[0] + bf_ref[...]
    lf = -_softplus(-z)
    tri = (lax.broadcasted_iota(I32, (ts, ts), 0) >= lax.broadcasted_iota(I32, (ts, ts), 1)).astype(F32)
    c = _dot(tri, lf, precision=HI) + carry_ref[0:1, :]
    c_ref[0] = c
    carry_ref[0:1, :] = c[ts - 1:ts, :]


def _fox_prep(f_logit, b_f, ts, cb=0):
    B, S, _ = f_logit.shape
    W = LANES
    return pl.pallas_call(
        _fox_prep_kernel,
        grid=(B, S // ts),
        in_specs=[pl.BlockSpec((1, ts, W), lambda b, j: (b, j, cb)), pl.BlockSpec((1, W), lambda b, j: (0, 0))],
        out_specs=pl.BlockSpec((1, ts, W), lambda b, j: (b, j, 0)),
        out_shape=jax.ShapeDtypeStruct((B, S, W), F32),
        scratch_shapes=[pltpu.VMEM((8, W), F32)],
        compiler_params=_params(("parallel", "arbitrary")),
        name="fox_prep",
    )(f_logit, b_f)


def _fox_kernel(q_ref, k_ref, v_ref, ccol_ref, crow_ref, o_ref):
    tq = q_ref.shape[1]
    i = pl.program_id(2)
    scale = FOX_HEAD_DIM ** -0.5
    q = q_ref[0]
    cq = ccol_ref[0, 0]
    rc = lax.broadcasted_iota(I32, (tq, tq), 0) - lax.broadcasted_iota(I32, (tq, tq), 1)

    def scores(n):
        start = pl.multiple_of(n * tq, tq)
        s = _dot(q, k_ref[0, pl.ds(start, tq), :], _NT) * scale + cq - crow_ref[0, 0, n]
        return s, v_ref[0, pl.ds(start, tq), :]

    s, vblk = scores(i)
    s = jnp.where(rc >= 0, s, -jnp.inf)
    m0 = jnp.max(s, axis=1, keepdims=True)
    p = jnp.exp(s - m0)
    l0 = jnp.sum(p, axis=1, keepdims=True)
    acc0 = _dot(p.astype(BF16), vblk)

    def body(n, carry):
        m, l, acc = carry
        s, vblk = scores(n)
        m_new = jnp.maximum(m, jnp.max(s, axis=1, keepdims=True))
        alpha = jnp.exp(m - m_new)
        p = jnp.exp(s - m_new)
        l = alpha * l + jnp.sum(p, axis=1, keepdims=True)
        acc = alpha * acc + _dot(p.astype(BF16), vblk)
        return m_new, l, acc

    _, l, acc = lax.fori_loop(0, i, body, (m0, l0, acc0))
    o_ref[0] = (acc / l).astype(o_ref.dtype)


def _fox(qkv, c, tq, cb0=0):
    B, S, _ = qkv.shape
    H, d = FOX_HEADS, FOX_HEAD_DIM
    nq = S // tq
    ch = jnp.transpose(c[:, :, :H], (0, 2, 1))
    c_col = ch[..., None]
    c_row = ch.reshape(B, H, nq, 1, tq)
    return pl.pallas_call(
        _fox_kernel,
        grid=(B, H, nq),
        in_specs=[
            pl.BlockSpec((1, tq, d), lambda b, h, i: (b, i, cb0 + h)),
            pl.BlockSpec((1, S, d), lambda b, h, i: (b, 0, cb0 + H + h)),
            pl.BlockSpec((1, S, d), lambda b, h, i: (b, 0, cb0 + 2 * H + h)),
            pl.BlockSpec((1, 1, tq, 1), lambda b, h, i: (b, h, i, 0)),
            pl.BlockSpec((1, 1, nq, 1, tq), lambda b, h, i: (b, h, 0, 0, 0)),
        ],
        out_specs=pl.BlockSpec((1, tq, d), lambda b, h, i: (b, i, h)),
        out_shape=jax.ShapeDtypeStruct((B, S, H * d), BF16),
        compiler_params=_params(("parallel", "parallel", "arbitrary")),
        name="fox",
    )(qkv, qkv, qkv, c_col, c_row)


def _ret_kernel(q_ref, k_ref, v_ref, g_ref, lng_ref, lnb_ref, o_ref, state_ref):
    C = q_ref.shape[1]
    dk, dv = RET_KEY_DIM, RET_VALUE_DIM

    @pl.when(pl.program_id(1) == 0)
    def _():
        state_ref[...] = jnp.zeros_like(state_ref)

    row = lax.broadcasted_iota(I32, (C, C), 0)
    col = lax.broadcasted_iota(I32, (C, C), 1)
    diff = (row - col).astype(F32)
    tcol = lax.broadcasted_iota(I32, (C, 1), 0).astype(F32)
    scale = dk ** -0.5
    for h in range(RET_HEADS):
        log_gamma = float(np.log(1.0 - 2.0 ** (-5.0 - h)))
        intra = jnp.where(diff >= 0, jnp.exp(log_gamma * jnp.maximum(diff, 0.0)), 0.0)
        q_dec = jnp.exp(log_gamma * (tcol + 1.0))
        k_dec = jnp.exp(log_gamma * (C - 1.0 - tcol))
        blk_dec = float(np.exp(log_gamma * C))
        qh = q_ref[0, :, h * dk:(h + 1) * dk]
        kh = (k_ref[0, :, h * dk:(h + 1) * dk].astype(F32) * scale).astype(BF16)
        vh = v_ref[0, :, h * dv:(h + 1) * dv]
        R = state_ref[h]
        inner = _dot(qh, kh, _NT) * intra
        o = _dot(inner.astype(BF16), vh) + _dot(qh, R.astype(BF16)) * q_dec
        state_ref[h] = R * blk_dec + _dot((kh.astype(F32) * k_dec).astype(BF16), vh, _TN)
        mu = jnp.mean(o, axis=1, keepdims=True)
        d = o - mu
        var = jnp.mean(d * d, axis=1, keepdims=True)
        y = d * lax.rsqrt(var + RET_GN_EPS) * lng_ref[:, h * dv:(h + 1) * dv] + lnb_ref[:, h * dv:(h + 1) * dv]
        g = g_ref[0, :, h * dv:(h + 1) * dv].astype(F32)
        o_ref[0, :, h * dv:(h + 1) * dv] = (g * _sigmoid(g) * y).astype(o_ref.dtype)


def _retention(ret, ln_g, ln_b, C, col0=0):
    B, S, _ = ret.shape
    qk = RET_HEADS * RET_KEY_DIM
    assert col0 % RET_W == 0
    qb, vb = col0 // qk, col0 // RET_W
    return pl.pallas_call(
        _ret_kernel,
        grid=(B, S // C),
        in_specs=[
            pl.BlockSpec((1, C, qk), lambda b, c: (b, c, qb)),
            pl.BlockSpec((1, C, qk), lambda b, c: (b, c, qb + 1)),
            pl.BlockSpec((1, C, RET_W), lambda b, c: (b, c, vb + 1)),
            pl.BlockSpec((1, C, RET_W), lambda b, c: (b, c, vb + 2)),
            pl.BlockSpec((1, RET_W), lambda b, c: (0, 0)),
            pl.BlockSpec((1, RET_W), lambda b, c: (0, 0)),
        ],
        out_specs=pl.BlockSpec((1, C, RET_W), lambda b, c: (b, c, 0)),
        out_shape=jax.ShapeDtypeStruct((B, S, RET_W), BF16),
        scratch_shapes=[pltpu.VMEM((RET_HEADS, RET_KEY_DIM, RET_VALUE_DIM), F32)],
        compiler_params=_params(("parallel", "arbitrary")),
        name="retention",
    )(ret, ret, ret, ret, ln_g, ln_b)


def _rwkv_prep_kernel(x_ref, mu_ref, w0_ref, wup_ref, a0_ref, aup_ref, gup_ref, kk_ref, ka_ref, rk_ref, hsum_ref,
                      r_o, lw_o, k_o, v_o, a_o, b_o, g_o, bonus_o, carry_ref):
    ts = x_ref.shape[1]
    W = RWKV_W

    @pl.when(pl.program_id(1) == 0)
    def _():
        carry_ref[...] = jnp.zeros_like(carry_ref)

    x = x_ref[0]
    row = lax.broadcasted_iota(I32, x.shape, 0)
    shifted = jnp.where(row == 0, carry_ref[0:1, :], pltpu.roll(x, 1, 0))
    carry_ref[0:1, :] = x[ts - 1:ts, :]
    xs = x + (shifted - x) * mu_ref[...]
    r = xs[:, 0:W]
    k = xs[:, W:2 * W]
    v = xs[:, 2 * W:3 * W]
    wd = xs[:, 3 * W:3 * W + LANES]
    ad = xs[:, 3 * W + LANES:3 * W + 2 * LANES]
    gd = xs[:, 3 * W + 2 * LANES:3 * W + 3 * LANES]
    w_log = -_softplus(-(w0_ref[...] + _dot(jnp.tanh(wd), wup_ref[...], precision=HI))) - 0.5
    a = _sigmoid(a0_ref[...] + _dot(ad, aup_ref[...], precision=HI))
    g = _dot(_sigmoid(gd), gup_ref[...], precision=HI)
    kk = k * kk_ref[...]
    norm = jnp.sqrt(_dot(kk * kk, hsum_ref[...], precision=HI))
    kk = kk / jnp.maximum(norm, 1e-12)
    k2 = k * (1.0 + (a - 1.0) * ka_ref[...])
    r_o[0] = r
    lw_o[0] = -jnp.exp(w_log)
    k_o[0] = k2
    v_o[0] = v
    a_o[0] = -kk
    b_o[0] = kk * a
    g_o[0] = g
    bonus_o[0] = _dot(r * k2 * rk_ref[...], hsum_ref[...], precision=HI) * v


def _rwkv_prep(cols, mu, w0, wup, a0, aup, gup, k_k, k_a, r_k, hsum, ts):
    B, S, Wc = cols.shape
    W = RWKV_W
    vec = lambda n: pl.BlockSpec((1, n), lambda b, j: (0, 0))
    mat = lambda r, c: pl.BlockSpec((r, c), lambda b, j: (0, 0))
    out = jax.ShapeDtypeStruct((B, S, W), F32)
    return pl.pallas_call(
        _rwkv_prep_kernel,
        grid=(B, S // ts),
        in_specs=[pl.BlockSpec((1, ts, Wc), lambda b, j: (b, j, 0)), vec(Wc), vec(W), mat(LANES, W), vec(W),
                  mat(LANES, W), mat(LANES, W), vec(W), vec(W), vec(W), mat(W, W)],
        out_specs=[pl.BlockSpec((1, ts, W), lambda b, j: (b, j, 0))] * 8,
        out_shape=[out] * 8,
        scratch_shapes=[pltpu.VMEM((8, Wc), F32)],
        compiler_params=_params(("parallel", "arbitrary"), V7X_VMEM_LIMIT),
        name="rwkv_prep",
    )(cols, mu, w0, wup, a0, aup, gup, k_k, k_a, r_k, hsum)


def _rwkv_scan_kernel(r_ref, lw_ref, k_ref, v_ref, a_ref, b_ref, g_ref, bonus_ref, lng_ref, lnb_ref, hmean_ref,
                      o_ref, state_ref, y_ref):
    C = RWKV_CHUNK
    N = RWKV_HEAD_DIM
    TS = r_ref.shape[1]

    @pl.when(pl.program_id(1) == 0)
    def _():
        state_ref[...] = jnp.zeros_like(state_ref)

    row = lax.broadcasted_iota(I32, (C, C), 0)
    col = lax.broadcasted_iota(I32, (C, C), 1)
    incl = row >= col
    strict = row > col
    eye = (row == col).astype(F32)
    tri = incl.astype(F32)

    def chunk(c, carry):
        sl = pl.ds(pl.multiple_of(c * C, C), C)
        lw = lw_ref[0, sl, :]
        cum = _dot(tri, lw, precision=HI)
        cum_last = cum[C - 1:C, :]
        e_pos = jnp.exp(cum)
        e_neg = jnp.exp(-cum)
        e_last = jnp.exp(cum_last - cum)
        rt = r_ref[0, sl, :] * e_pos
        at = a_ref[0, sl, :] * jnp.exp(cum - lw)
        b = b_ref[0, sl, :]
        k = k_ref[0, sl, :]
        v = v_ref[0, sl, :]
        bt = b * e_neg
        kt = k * e_neg
        bl = b * e_last
        kl = k * e_last
        pc = jnp.exp(cum_last)
        heads = range(RWKV_HEADS)
        sl_h = [slice(h * N, (h + 1) * N) for h in heads]
        at_h = [at[:, s].astype(BF16) for s in sl_h]
        rt_h = [rt[:, s] for s in sl_h]
        v_h = [v[:, s].astype(BF16) for s in sl_h]
        bt_h = [bt[:, s].astype(BF16) for s in sl_h]
        kt_h = [kt[:, s].astype(BF16) for s in sl_h]
        bl_h = [bl[:, s].astype(BF16) for s in sl_h]
        kl_h = [kl[:, s].astype(BF16) for s in sl_h]
        lhs2 = [jnp.concatenate([at_h[h], rt_h[h].astype(BF16)], axis=0) for h in heads]
        sb = [_dot(lhs2[h], bt_h[h], _NT) for h in heads]
        sk = [_dot(lhs2[h], kt_h[h], _NT) for h in heads]
        a_ab = [jnp.where(strict, sb[h][:C], 0.0) for h in heads]
        m_rb = [jnp.where(incl, sb[h][C:], 0.0).astype(BF16) for h in heads]
        a_ak = [jnp.where(strict, sk[h][:C], 0.0).astype(BF16) for h in heads]
        m_rk = [jnp.where(incl, sk[h][C:], 0.0).astype(BF16) for h in heads]
        xb = [a_ab[h].astype(BF16) for h in heads]
        t_inv = [eye + a_ab[h] for h in heads]
        akv = [_dot(a_ak[h], v_h[h]).astype(BF16) for h in heads]
        mrkv = [_dot(m_rk[h], v_h[h]) for h in heads]
        klv = [_dot(kl_h[h], v_h[h], _TN) for h in heads]
        x = [_dot(xb[h], xb[h]) for h in heads]
        for step in range(int(np.log2(C)) - 1):
            xb = [x[h].astype(BF16) for h in heads]
            if step < int(np.log2(C)) - 2:
                x = [_dot(xb[h], xb[h]) for h in heads]
            t_inv = [t_inv[h] + _dot(t_inv[h].astype(BF16), xb[h]) for h in heads]
        t_b = [t_inv[h].astype(BF16) for h in heads]
        abar = [_dot(t_b[h], at_h[h]).astype(BF16) for h in heads]
        u0 = [_dot(t_b[h], akv[h]).astype(BF16) for h in heads]
        rbar = [(rt_h[h] + _dot(m_rb[h], abar[h])).astype(BF16) for h in heads]
        y0 = [_dot(m_rb[h], u0[h]) + mrkv[h] for h in heads]
        gmat = [(eye * pc[:, sl_h[h]] + _dot(bl_h[h], abar[h], _TN)).astype(BF16) for h in heads]
        hadd = [_dot(bl_h[h], u0[h], _TN) + klv[h] for h in heads]
        h0b = [state_ref[h].astype(BF16) for h in heads]
        for h in heads:
            y_ref[:, sl_h[h]] = _dot(rbar[h], h0b[h]) + y0[h]
        for h in heads:
            state_ref[h] = _dot(gmat[h], h0b[h]) + hadd[h]
        y = y_ref[...]
        mu = _dot(y, hmean_ref[...], precision=HI)
        d = y - mu
        var = _dot(d * d, hmean_ref[...], precision=HI)
        yn = d * lax.rsqrt(var + RWKV_GN_EPS) * lng_ref[...] + lnb_ref[...]
        o_ref[0, sl, :] = ((yn + bonus_ref[0, sl, :]) * g_ref[0, sl, :]).astype(o_ref.dtype)
        return carry

    lax.fori_loop(0, TS // C, chunk, 0)


def _rwkv_scan(r, lw, k, v, a, b, g, bonus, ln_g, ln_b, hmean, ts):
    B, S, W = r.shape
    blk = pl.BlockSpec((1, ts, W), lambda bb, j: (bb, j, 0))
    vec = pl.BlockSpec((1, W), lambda bb, j: (0, 0))
    return pl.pallas_call(
        _rwkv_scan_kernel,
        grid=(B, S // ts),
        in_specs=[blk] * 8 + [vec, vec, pl.BlockSpec((W, W), lambda bb, j: (0, 0))],
        out_specs=blk,
        out_shape=jax.ShapeDtypeStruct((B, S, W), BF16),
        scratch_shapes=[pltpu.VMEM((RWKV_HEADS, RWKV_HEAD_DIM, RWKV_HEAD_DIM), F32),
                        pltpu.VMEM((RWKV_CHUNK, W), F32)],
        compiler_params=_params(("parallel", "arbitrary"), V7X_VMEM_LIMIT),
        name="rwkv_scan",
    )(r, lw, k, v, a, b, g, bonus, ln_g, ln_b, hmean)


def _layer_norm_rows(z, g, b):
    mu = jnp.mean(z, axis=1, keepdims=True)
    d = z - mu
    var = jnp.mean(d * d, axis=1, keepdims=True)
    return d * lax.rsqrt(var + LN_EPS) * g + b


def _pack_bf16_pairs(lo, hi):
    bits = lambda a: pltpu.bitcast(a.astype(BF16).astype(F32), U32)
    return (bits(lo) >> 16) | (bits(hi) & jnp.uint32(0xFFFF0000))


def _outproj_kernel(ya_ref, yb_ref, yc_ref, yd_ref, h_ref, w_ref, g_ref, b_ref, rw_ref, rb_ref,
                    h1_ref, hp_ref, ti_ref, tg_ref):
    mix = _dot(ya_ref[...], w_ref[0]) + _dot(yb_ref[...], w_ref[1]) + _dot(yc_ref[...], w_ref[2]) + _dot(yd_ref[...], w_ref[3])
    hn = _layer_norm_rows(DEEPNORM_ALPHA * h_ref[...] + mix, g_ref[...], b_ref[...])
    h1_ref[...] = hn
    hp_ref[...] = _pack_bf16_pairs(hn[:, :hn.shape[1] // 2], hn[:, hn.shape[1] // 2:])
    tm = hn.shape[0]
    lane = lax.broadcasted_iota(I32, (tm, LANES), 1)
    lane_f = lane.astype(F32)
    hn_hi = hn.astype(BF16)
    hn_lo = (hn - hn_hi.astype(F32)).astype(BF16)
    both = _dot(hn_hi, rw_ref[...])
    logits = both[:, :LANES] + (both[:, LANES:] + _dot(hn_lo, rw_ref[:, :LANES])) + rb_ref[...]
    logits = jnp.where(lane < N_EXPERTS, logits, -jnp.inf)
    top = _topk_lanes(logits, lane_f, TOP_K)
    v0 = top[0][0]
    es = [jnp.exp(vk - v0) for vk, _ in top]
    denom = es[0] + es[1] + es[2] + es[3]
    ti = jnp.zeros((tm, LANES), F32)
    tg = jnp.zeros((tm, LANES), F32)
    for kk in range(TOP_K):
        ti = jnp.where(lane == kk, top[kk][1], ti)
        tg = jnp.where(lane == kk, es[kk] / denom, tg)
    ti_ref[...] = ti.astype(I32)
    tg_ref[...] = tg


def _outproj_ln_router(ys, h, w_out4, ln_g, ln_b, rw, rb, tm):
    T, D = h.shape
    Wg = ys[0].shape[1]
    yspec = pl.BlockSpec((tm, Wg), lambda i: (i, 0))
    row = pl.BlockSpec((tm, D), lambda i: (i, 0))
    vec = lambda n: pl.BlockSpec((1, n), lambda i: (0, 0))
    small = pl.BlockSpec((tm, LANES), lambda i: (i, 0))
    return pl.pallas_call(
        _outproj_kernel,
        grid=(T // tm,),
        in_specs=[yspec] * 4 + [row, pl.BlockSpec((4, Wg, D), lambda i: (0, 0, 0)), vec(D), vec(D),
                                pl.BlockSpec((D, 2 * LANES), lambda i: (0, 0)), vec(LANES)],
        out_specs=[row, pl.BlockSpec((tm, D // 2), lambda i: (i, 0)), small, small],
        out_shape=[jax.ShapeDtypeStruct((T, D), F32), jax.ShapeDtypeStruct((T, D // 2), U32),
                   jax.ShapeDtypeStruct((T, LANES), I32), jax.ShapeDtypeStruct((T, LANES), F32)],
        compiler_params=_params(("parallel",), V7X_VMEM_LIMIT),
        name="outproj_ln_router",
    )(*ys, h, w_out4, ln_g, ln_b, rw, rb)


def _unpack_bf16_pairs(w):
    return pltpu.bitcast(w << 16, F32), pltpu.bitcast(w & jnp.uint32(0xFFFF0000), F32)


def _moe_kernel(te_ref, nu_ref, src0_ref, srcn_ref, dstp_ref, hp_hbm, w1g_ref, w1l_ref, b1g_ref, b1l_ref, w2a_ref,
                w2b_ref, b2a_ref, b2b_ref, y_hbm, xu_ref, xb_ref, act_ref, acc_ref, gsem, ssem, *, n_real_rows):
    i = pl.program_id(0)
    j = pl.program_id(1)
    ns = MOE_STEPS
    nj = 2 * ns
    tm, half = xu_ref.shape
    rows_per_step = tm // nj
    th = act_ref.shape[1] // ns
    tn = half // ns
    nu = nu_ref[0]
    valid = i < nu
    slot = i % 2
    base = j * rows_per_step

    def wait_gather():
        pltpu.make_async_copy(hp_hbm.at[pl.ds(0, tm), :], xu_ref, gsem).wait()

    def wait_scatter():
        pltpu.make_async_copy(acc_ref.at[0], y_hbm.at[pl.ds(0, tm), :], ssem).wait()

    def issue_gather_chunk():
        for r in range(rows_per_step):
            t = srcn_ref[0, 0, base + r]
            pltpu.make_async_copy(hp_hbm.at[pl.ds(t, 1), :], xu_ref.at[pl.ds(base + r, 1), :], gsem).start()

    def issue_scatter_chunk():
        for r in range(rows_per_step):
            t = dstp_ref[0, 0, base + r]
            pltpu.make_async_copy(acc_ref.at[1 - slot, pl.ds(base + r, 1), :], y_hbm.at[pl.ds(t, 1), :], ssem).start()

    @pl.when((i == 0) & (j == 0))
    def _():
        acc_ref[1] = jnp.zeros(acc_ref.shape[1:], U32)
        fills = [pltpu.make_async_copy(acc_ref.at[1], y_hbm.at[pl.ds(n_real_rows + f * tm, tm), :], ssem)
                 for f in range((y_hbm.shape[0] - n_real_rows) // tm)]
        for fill in fills:
            fill.start()
        for fill in fills:
            fill.wait()

        def issue(r, carry):
            t = src0_ref[0, 0, r]
            pltpu.make_async_copy(hp_hbm.at[pl.ds(t, 1), :], xu_ref.at[pl.ds(r, 1), :], gsem).start()
            return carry

        lax.fori_loop(0, tm, issue, 0)

    @pl.when((j == 0) & (i <= nu))
    def _():
        wait_gather()

    @pl.when(valid & (j == 0))
    def _():
        lo, hi = _unpack_bf16_pairs(xu_ref[...])
        xb_ref[:, :half] = lo.astype(BF16)
        xb_ref[:, half:] = hi.astype(BF16)

    @pl.when(valid & (j < ns))
    def _():
        issue_gather_chunk()
        issue_scatter_chunk()
        xb = xb_ref[...]
        w1g = w1g_ref[0].astype(BF16)
        w1l = w1l_ref[0].astype(BF16)
        b1g = b1g_ref[0]
        b1l = b1l_ref[0]
        acts = []
        for c in range(2):
            cs = slice(c * (th // 2), (c + 1) * (th // 2))
            glu = jnp.minimum(_dot(xb, w1g[:, cs]) + b1g[:, cs], SWIGLU_LIMIT)
            lin = jnp.clip(_dot(xb, w1l[:, cs]) + b1l[:, cs], -SWIGLU_LIMIT, SWIGLU_LIMIT)
            acts.append((glu * _sigmoid(SWIGLU_ALPHA * glu) * (lin + 1.0)).astype(BF16))
        act = jnp.concatenate(acts, axis=1)
        for jj in range(ns):
            @pl.when(j == jj)
            def _():
                act_ref[:, jj * th:(jj + 1) * th] = act

    @pl.when(valid & (j >= ns))
    def _():
        act = act_ref[...]
        lo = _dot(act, w2a_ref[0].astype(BF16)) + b2a_ref[0]
        issue_gather_chunk()
        hi = _dot(act, w2b_ref[0].astype(BF16)) + b2b_ref[0]
        issue_scatter_chunk()
        packed = _pack_bf16_pairs(lo, hi)
        for nn in range(ns):
            @pl.when(j == ns + nn)
            def _():
                acc_ref[slot, :, nn * tn:(nn + 1) * tn] = packed

        @pl.when(j == nj - 1)
        def _():
            wait_scatter()

    @pl.when(i == nu)
    def _():
        issue_scatter_chunk()

        @pl.when(j == nj - 1)
        def _():
            wait_scatter()


def _moe_ffn(hp, src_tok, dst_row, tile_expert, n_used, w1, b1, w2, b2, tm):
    NT = src_tok.shape[0]
    _, D, _ = w1.shape
    ns = MOE_STEPS
    nj = 2 * ns
    half = D // 2
    th = D_EXPERT // ns
    tn = half // ns
    assert tm % nj == 0

    def step(i, j, nu):
        return jnp.where(i < nu[0], j, nj - 1)

    def j1(i, j, te, nu):
        return jnp.minimum(step(i, j, nu), ns - 1)

    def j2(i, j, te, nu):
        return jnp.maximum(step(i, j, nu) - ns, 0)

    smem = lambda f: pl.BlockSpec((1, 1, tm), f, memory_space=pltpu.SMEM)
    assert (NT * tm - hp.shape[0] * TOP_K) % tm == 0
    return pl.pallas_call(
        functools.partial(_moe_kernel, n_real_rows=hp.shape[0] * TOP_K),
        grid_spec=pltpu.PrefetchScalarGridSpec(
            num_scalar_prefetch=2,
            grid=(NT, nj),
            in_specs=[
                smem(lambda i, j, te, nu: (0, 0, 0)),
                smem(lambda i, j, te, nu: (jnp.minimum(i + 1, NT - 1), 0, 0)),
                smem(lambda i, j, te, nu: (jnp.maximum(i - 1, 0), 0, 0)),
                pl.BlockSpec(memory_space=pl.ANY),
                pl.BlockSpec((1, D, th), lambda i, j, te, nu: (te[i], 0, j1(i, j, te, nu))),
                pl.BlockSpec((1, D, th), lambda i, j, te, nu: (te[i], 0, ns + j1(i, j, te, nu))),
                pl.BlockSpec((1, 1, th), lambda i, j, te, nu: (te[i], 0, j1(i, j, te, nu))),
                pl.BlockSpec((1, 1, th), lambda i, j, te, nu: (te[i], 0, ns + j1(i, j, te, nu))),
                pl.BlockSpec((1, D_EXPERT, tn), lambda i, j, te, nu: (te[i], 0, j2(i, j, te, nu))),
                pl.BlockSpec((1, D_EXPERT, tn), lambda i, j, te, nu: (te[i], 0, ns + j2(i, j, te, nu))),
                pl.BlockSpec((1, 1, tn), lambda i, j, te, nu: (te[i], 0, j2(i, j, te, nu))),
                pl.BlockSpec((1, 1, tn), lambda i, j, te, nu: (te[i], 0, ns + j2(i, j, te, nu))),
            ],
            out_specs=pl.BlockSpec(memory_space=pl.ANY),
            scratch_shapes=[pltpu.VMEM((tm, half), U32), pltpu.VMEM((tm, D), BF16), pltpu.VMEM((tm, D_EXPERT), BF16),
                            pltpu.VMEM((2, tm, half), U32), pltpu.SemaphoreType.DMA, pltpu.SemaphoreType.DMA],
        ),
        out_shape=jax.ShapeDtypeStruct((NT * tm, half), U32),
        compiler_params=_params(("arbitrary", "arbitrary"), V7X_VMEM_LIMIT_MOE),
        name="moe_ffn",
    )(tile_expert, n_used, src_tok, src_tok, dst_row, hp, w1, w1, b1, b1, w2, w2, b2, b2)


def _combine_kernel(y0_ref, y1_ref, y2_ref, y3_ref, tg_ref, h_ref, g_ref, b_ref, o_ref):
    tg = tg_ref[...]
    lo = hi = None
    for kk, y_ref in enumerate((y0_ref, y1_ref, y2_ref, y3_ref)):
        ylo, yhi = _unpack_bf16_pairs(y_ref[...])
        gk = tg[:, kk:kk + 1]
        lo = gk * ylo if lo is None else lo + gk * ylo
        hi = gk * yhi if hi is None else hi + gk * yhi
    ffn = jnp.concatenate([lo, hi], axis=1)
    o_ref[...] = _layer_norm_rows(DEEPNORM_ALPHA * h_ref[...] + ffn, g_ref[...], b_ref[...])


def _combine_ln(y, tg, h1, ln_g, ln_b, tq):
    T, D = h1.shape
    row = pl.BlockSpec((tq, D), lambda i: (i, 0))
    vec = pl.BlockSpec((1, D), lambda i: (0, 0))
    nq = T // tq
    ys = [pl.BlockSpec((tq, D // 2), functools.partial(lambda i, kk: (kk * nq + i, 0), kk=kk)) for kk in range(TOP_K)]
    return pl.pallas_call(
        _combine_kernel,
        grid=(nq,),
        in_specs=ys + [pl.BlockSpec((tq, LANES), lambda i: (i, 0)), row, vec, vec],
        out_specs=row,
        out_shape=jax.ShapeDtypeStruct((T, D), F32),
        compiler_params=_params(("parallel",), V7X_VMEM_LIMIT),
        name="combine_ln",
    )(y, y, y, y, tg, h1, ln_g, ln_b)


def _dispatch_tables(top_idx, tm, n_tiles):
    T = top_idx.shape[0]
    P = T * TOP_K
    E = N_EXPERTS
    R = n_tiles * tm
    shift = int(np.ceil(np.log2(R)))
    assert (E + 1) << shift < 2 ** 31
    e = top_idx.reshape(P)
    experts = jnp.arange(E, dtype=I32)
    counts = jnp.sum((e[:, None] == experts[None, :]).astype(I32), axis=0)
    tiles = (counts + tm - 1) // tm
    tile_end = jnp.cumsum(tiles)
    n_used = tile_end[-1]
    pad_end = jnp.cumsum(tiles * tm - counts)
    d = jnp.arange(R - P, dtype=I32)
    pad_expert = jnp.sum((pad_end[None, :] <= d[:, None]).astype(I32), axis=1)
    keys = jnp.concatenate([(e << shift) + jnp.arange(P, dtype=I32), (pad_expert << shift) + P + d])
    ids = jnp.sort(keys) & ((1 << shift) - 1)
    real = ids < P
    src_tok = jnp.where(real, ids // TOP_K, 0).astype(I32)
    dst_row = jnp.where(real, (ids % TOP_K) * T + ids // TOP_K, ids).astype(I32)
    tile_ids = jnp.arange(n_tiles, dtype=I32)
    last = jnp.maximum(n_used - 1, 0)
    te = jnp.sum((tile_end[None, :] <= jnp.minimum(tile_ids, last)[:, None]).astype(I32), axis=1)
    return src_tok, dst_row, te.astype(I32), n_used.reshape(1).astype(I32)


def _pad_cols(a, n):
    return jnp.pad(a, ((0, 0), (0, n - a.shape[1])))


def _pad_rows(a, n):
    return jnp.pad(a, ((0, n - a.shape[0]), (0, 0)))


def _rwkv_col_layout(a, tail=None):
    W = RWKV_W
    rkv, wd, ad, gd = a[:, :3 * W], a[:, 3 * W:3 * W + 64], a[:, 3 * W + 64:3 * W + 128], a[:, 3 * W + 128:]
    if tail is None:
        tail = jnp.zeros((a.shape[0], LANES), a.dtype)
    return jnp.concatenate([rkv, _pad_cols(wd, LANES), _pad_cols(ad, LANES), gd, tail], axis=1)


def _layer(layer, h, B, S, w_in, mu, w0, w_up, a0, a_up, g_up, k_k, k_a, r_k, rln_g, rln_b, fox_b_f, ret_ln_g, ret_ln_b,
           w_out, ln1_g, ln1_b, router_w, router_b, w1, b1, w2, b2, ln2_g, ln2_b):
    T, D = h.shape
    o0 = MOBA_COLS
    o1 = o0 + RWKV_SHIFT_COLS
    o2 = o1 + FOX_COLS
    w_att = jnp.concatenate([w_in[:, :o0], w_in[:, o1:o1 + 3 * FOX_W], w_in[:, o2:]], axis=1).astype(BF16)
    w_f32 = _rwkv_col_layout(w_in[:, o0:o1], _pad_cols(w_in[:, o1 + 3 * FOX_W:o2], LANES)).astype(BF16)
    fox_cb0 = MOBA_COLS // FOX_HEAD_DIM
    ret_col0 = MOBA_COLS + 3 * FOX_W
    foxf_cb = w_f32.shape[1] // LANES - 1

    tm = min(512, T)
    att = _matmul(h, w_att, BF16, tm, MOBA_COLS).reshape(B, S, -1)
    rwkv_cols = _matmul(h, w_f32, F32, tm, 1024).reshape(B, S, -1)

    slopes = jnp.exp2(-8.0 * jnp.arange(1, MOBA_HEADS + 1, dtype=F32) / MOBA_HEADS)
    y_a = _moba(att, slopes, min(512, S))

    head_of = np.arange(RWKV_W) // RWKV_HEAD_DIM
    same = (head_of[:, None] == head_of[None, :]).astype(np.float32)
    hsum = jnp.asarray(same)
    hmean = jnp.asarray(same / RWKV_HEAD_DIM)
    row = lambda a: a.reshape(1, -1)
    ts_r = min(256, S)
    prep = _rwkv_prep(rwkv_cols, _rwkv_col_layout(row(mu)), row(w0), _pad_rows(w_up, LANES), row(a0),
                      _pad_rows(a_up, LANES), g_up, row(k_k), row(k_a), row(r_k), hsum, ts_r)
    y_b = _rwkv_scan(*prep, row(rln_g), row(rln_b), hmean, min(512, S))

    c = _fox_prep(rwkv_cols, _pad_cols(row(fox_b_f), LANES), min(512, S), foxf_cb)
    y_c = _fox(att, c, min(512, S), fox_cb0)

    y_d = _retention(att, row(ret_ln_g), row(ret_ln_b), min(256, S), ret_col0)

    Wg = D // 4
    ys = [y.reshape(T, Wg) for y in (y_a, y_b, y_c, y_d)]
    rw32 = _pad_cols(router_w, LANES)
    rw_hi = rw32.astype(BF16)
    rw = jnp.concatenate([rw_hi, (rw32 - rw_hi.astype(F32)).astype(BF16)], axis=1)
    rb = _pad_cols(row(router_b), LANES)
    h1, hp, top_i, top_g = _outproj_ln_router(ys, h, w_out.astype(BF16).reshape(4, Wg, D), row(ln1_g), row(ln1_b),
                                              rw, rb, min(256, T))

    tm_e = min(MOE_TILE_ROWS, T)
    n_tiles = (T * TOP_K) // tm_e + N_EXPERTS + 1
    src_tok, dst_row, te, n_used = _dispatch_tables(top_i[:, :TOP_K], tm_e, n_tiles)
    y = _moe_ffn(hp, src_tok.reshape(n_tiles, 1, tm_e), dst_row.reshape(n_tiles, 1, tm_e), te + layer * N_EXPERTS,
                 n_used, w1, b1, w2, b2, tm_e)
    return _combine_ln(y, top_g, h1, row(ln2_g), row(ln2_b), min(256, T))


def kernel(x, w_in, rwkv_mu, rwkv_w0, rwkv_w_up, rwkv_a0, rwkv_a_up, rwkv_g_up, rwkv_k_k, rwkv_k_a, rwkv_r_k, rwkv_ln_g, rwkv_ln_b, fox_b_f, ret_ln_g, ret_ln_b, w_out, ln1_g, ln1_b, router_w, router_b, exp_w1, exp_b1, exp_w2, exp_b2, ln2_g, ln2_b):
    B, S, D = x.shape
    h = x.reshape(B * S, D)
    LE = exp_w1.shape[0] * exp_w1.shape[1]
    w1 = exp_w1.reshape(LE, D, -1)
    b1 = exp_b1.reshape(LE, 1, -1)
    w2 = exp_w2.reshape(LE, -1, D)
    b2 = exp_b2.reshape(LE, 1, D)
    for l in range(DEPTH):
        h = _layer(l, h, B, S, w_in[l], rwkv_mu[l], rwkv_w0[l], rwkv_w_up[l], rwkv_a0[l], rwkv_a_up[l], rwkv_g_up[l],
                   rwkv_k_k[l], rwkv_k_a[l], rwkv_r_k[l].reshape(-1), rwkv_ln_g[l], rwkv_ln_b[l], fox_b_f[l],
                   ret_ln_g[l], ret_ln_b[l], w_out[l], ln1_g[l], ln1_b[l], router_w[l], router_b[l],
                   w1, b1, w2, b2, ln2_g[l], ln2_b[l])
    return h.reshape(B, S, D)
```

```python
import functools

import numpy as np
import jax
import jax.numpy as jnp
from jax import lax
from jax.experimental import pallas as pl
from jax.experimental.pallas import tpu as pltpu

F32 = jnp.float32
BF16 = jnp.bfloat16
I32 = jnp.int32
U32 = jnp.uint32
HI = lax.Precision.HIGHEST

D_MODEL = 2048
DEPTH = 2
MOBA_HEADS = 4
MOBA_HEAD_DIM = 128
MOBA_W = MOBA_HEADS * MOBA_HEAD_DIM
MOBA_BLOCK = 256
MOBA_TOPK = 3
RWKV_HEADS = 8
RWKV_HEAD_DIM = 64
RWKV_W = RWKV_HEADS * RWKV_HEAD_DIM
RWKV_W_LORA = 64
RWKV_A_LORA = 64
RWKV_G_LORA = 128
RWKV_GN_EPS = 64e-5
RWKV_CHUNK = 64
FOX_HEADS = 4
FOX_HEAD_DIM = 128
FOX_W = FOX_HEADS * FOX_HEAD_DIM
RET_HEADS = 4
RET_KEY_DIM = 64
RET_VALUE_DIM = 128
RET_W = RET_HEADS * RET_VALUE_DIM
RET_GN_EPS = 1e-5
MOBA_COLS = 3 * MOBA_W
RWKV_SHIFT_COLS = 3 * RWKV_W + RWKV_W_LORA + RWKV_A_LORA + RWKV_G_LORA
FOX_COLS = 3 * FOX_W + FOX_HEADS
RET_COLS = 2 * RET_HEADS * RET_KEY_DIM + 2 * RET_W
N_EXPERTS = 32
TOP_K = 4
D_EXPERT = D_MODEL
SWIGLU_ALPHA = 1.702
SWIGLU_LIMIT = 7.0
LN_EPS = 1e-5
DEEPNORM_ALPHA = (2 * DEPTH) ** 0.25

LANES = 128
V7X_VMEM_LIMIT = 56 * 1024 * 1024
V7X_VMEM_LIMIT_MOE = 60 * 1024 * 1024
MOE_TILE_ROWS = 1024
MOE_STEPS = 4

_NT = (((1,), (1,)), ((), ()))
_TN = (((0,), (0,)), ((), ()))


def _dot(a, b, dims=None, precision=None):
    if dims is None:
        dims = (((a.ndim - 1,), (0,)), ((), ()))
    return lax.dot_general(a, b, dims, precision=precision, preferred_element_type=F32)


def _sigmoid(x):
    return 1.0 / (1.0 + jnp.exp(-x))


def _softplus(x):
    return jnp.maximum(x, 0.0) + jnp.log(1.0 + jnp.exp(-jnp.abs(x)))


def _params(sem, vmem=None):
    return pltpu.CompilerParams(dimension_semantics=sem, vmem_limit_bytes=vmem)


def _mm_kernel(x_ref, w_ref, o_ref, xb_ref):
    @pl.when(pl.program_id(1) == 0)
    def _():
        xb_ref[...] = x_ref[...].astype(BF16)

    o_ref[...] = _dot(xb_ref[...], w_ref[...]).astype(o_ref.dtype)


def _win_prep_kernel(w_ref, att_ref, f32g_ref):
    x = w_ref[0]
    rows = x.shape[0]
    W = RWKV_W
    o0 = MOBA_COLS
    o1 = o0 + RWKV_SHIFT_COLS
    o2 = o1 + FOX_COLS
    zeros = lambda n: jnp.zeros((rows, n), F32)
    att = jnp.concatenate([x[:, :o0], x[:, o1:o1 + 3 * FOX_W], x[:, o2:o2 + RET_COLS]], axis=1)
    wd = x[:, o0 + 3 * W:o0 + 3 * W + RWKV_W_LORA]
    ad = x[:, o0 + 3 * W + RWKV_W_LORA:o0 + 3 * W + RWKV_W_LORA + RWKV_A_LORA]
    gd = x[:, o0 + 3 * W + RWKV_W_LORA + RWKV_A_LORA:o1]
    ff = x[:, o1 + 3 * FOX_W:o2]
    f32g = jnp.concatenate([x[:, o0:o0 + 3 * W], wd, zeros(LANES - RWKV_W_LORA), ad, zeros(LANES - RWKV_A_LORA), gd,
                            ff, zeros(LANES - FOX_HEADS)], axis=1)
    att_ref[...] = att.astype(BF16)
    f32g_ref[...] = f32g.astype(BF16)


def _prep_in_weights(w_in, layer, tk=256):
    _, D, C = w_in.shape
    n_att = MOBA_COLS + 3 * FOX_W + RET_COLS
    n_f32 = 3 * RWKV_W + 4 * LANES
    return pl.pallas_call(
        _win_prep_kernel,
        grid=(D // tk,),
        in_specs=[pl.BlockSpec((1, tk, C), lambda i: (layer, i, 0))],
        out_specs=[pl.BlockSpec((tk, n_att), lambda i: (i, 0)), pl.BlockSpec((tk, n_f32), lambda i: (i, 0))],
        out_shape=[jax.ShapeDtypeStruct((D, n_att), BF16), jax.ShapeDtypeStruct((D, n_f32), BF16)],
        compiler_params=_params(("parallel",), V7X_VMEM_LIMIT),
        name="win_prep",
    )(w_in)


def _matmul(x, w, out_dtype, tm, tn):
    T, K = x.shape
    N = w.shape[1]
    assert T % tm == 0 and N % tn == 0
    return pl.pallas_call(
        _mm_kernel,
        grid=(T // tm, N // tn),
        in_specs=[pl.BlockSpec((tm, K), lambda i, j: (i, 0)), pl.BlockSpec((K, tn), lambda i, j: (0, j))],
        out_specs=pl.BlockSpec((tm, tn), lambda i, j: (i, j)),
        out_shape=jax.ShapeDtypeStruct((T, N), out_dtype),
        scratch_shapes=[pltpu.VMEM((tm, K), BF16)],
        compiler_params=_params(("parallel", "arbitrary"), V7X_VMEM_LIMIT),
        name="in_proj",
    )(x, w)


def _topk_lanes(vals, lane_f, k):
    outs = []
    g = vals
    for _ in range(k):
        m = jnp.max(g, axis=1, keepdims=True)
        cand = jnp.where((g == m) & (g > -jnp.inf), lane_f, float(LANES))
        idx = jnp.min(cand, axis=1, keepdims=True)
        outs.append((m, idx))
        g = jnp.where(lane_f == idx, -jnp.inf, g)
    return outs


def _moba_kernel(slopes_ref, q_ref, k_ref, v_ref, o_ref, kmean_ref):
    L = MOBA_BLOCK
    S = k_ref.shape[1]
    nb = S // L
    h = pl.program_id(1)
    i = pl.program_id(2)
    scale = MOBA_HEAD_DIM ** -0.5

    @pl.when(i == 0)
    def _():
        kmean_ref[...] = jnp.zeros_like(kmean_ref)
        for n in range(nb):
            kb = k_ref[0, n * L:(n + 1) * L, :].astype(F32)
            kmean_ref[n:n + 1, :] = jnp.sum(kb, axis=0, keepdims=True) * (1.0 / L)

    TQ = q_ref.shape[1]
    G = TQ // L
    q = q_ref[0]
    lane = lax.broadcasted_iota(I32, (TQ, LANES), 1)
    lane_f = lane.astype(F32)
    log2_l = int(np.log2(L))
    row_sub = lax.broadcasted_iota(I32, (TQ, 1), 0) >> log2_l
    gate = _dot(q.astype(F32), kmean_ref[...], _NT, precision=HI)
    gate = jnp.where(lane < G * i + row_sub, gate, -jnp.inf)
    sel = jnp.zeros((TQ, LANES), F32)
    for _, idx in _topk_lanes(gate, lane_f, MOBA_TOPK):
        sel = jnp.where(lane_f == idx, 1.0, sel)

    slope = slopes_ref[h]
    r_i = lax.broadcasted_iota(I32, (TQ, TQ), 0)
    c_i = lax.broadcasted_iota(I32, (TQ, TQ), 1)
    bias0 = (-slope) * (r_i - c_i).astype(F32)
    col_sub = c_i >> log2_l

    def col_penalty(pens):
        out = pens[G - 1]
        for g in range(G - 2, -1, -1):
            out = jnp.where(col_sub == g, pens[g], out)
        return out

    def sel_pen(blk):
        picked = jnp.max(jnp.where(lane == blk, sel, 0.0), axis=1, keepdims=True)
        return jnp.where(picked > 0.0, 0.0, -jnp.inf)

    def tile(start, pens):
        s = _dot(q, k_ref[0, pl.ds(start, TQ), :], _NT) * scale + bias0 + col_penalty(pens)
        return s, v_ref[0, pl.ds(start, TQ), :]

    own = pl.multiple_of(i * TQ, TQ)
    s, vblk = tile(own, [jnp.where(row_sub > g, sel_pen(G * i + g), 0.0) for g in range(G)])
    s = jnp.where(c_i <= r_i, s, -jnp.inf)
    m0 = jnp.max(s, axis=1, keepdims=True)
    p = jnp.exp(s - m0)
    l0 = jnp.sum(p, axis=1, keepdims=True)
    acc0 = _dot(p.astype(BF16), vblk)

    def body(n, carry):
        m, l, acc = carry
        start = pl.multiple_of(n * TQ, TQ)
        off = (-slope) * ((i - n) * TQ).astype(F32)
        s, vblk = tile(start, [sel_pen(G * n + g) + off for g in range(G)])
        m_new = jnp.maximum(m, jnp.max(s, axis=1, keepdims=True))
        alpha = jnp.exp(m - m_new)
        p = jnp.exp(s - m_new)
        l = alpha * l + jnp.sum(p, axis=1, keepdims=True)
        acc = alpha * acc + _dot(p.astype(BF16), vblk)
        return m_new, l, acc

    _, l, acc = lax.fori_loop(0, i, body, (m0, l0, acc0))
    o_ref[0] = (acc / l).astype(o_ref.dtype)


def _moba(qkv, slopes, tq, cb0=0):
    B, S, _ = qkv.shape
    H, L, d = MOBA_HEADS, MOBA_BLOCK, MOBA_HEAD_DIM
    assert S % tq == 0 and tq % L == 0 and S // L <= LANES
    return pl.pallas_call(
        _moba_kernel,
        grid_spec=pltpu.PrefetchScalarGridSpec(
            num_scalar_prefetch=1,
            grid=(B, H, S // tq),
            in_specs=[
                pl.BlockSpec((1, tq, d), lambda b, h, i, sl: (b, i, cb0 + h)),
                pl.BlockSpec((1, S, d), lambda b, h, i, sl: (b, 0, cb0 + H + h)),
                pl.BlockSpec((1, S, d), lambda b, h, i, sl: (b, 0, cb0 + 2 * H + h)),
            ],
            out_specs=pl.BlockSpec((1, tq, d), lambda b, h, i, sl: (b, i, h)),
            scratch_shapes=[pltpu.VMEM((LANES, d), F32)],
        ),
        out_shape=jax.ShapeDtypeStruct((B, S, H * d), BF16),
        compiler_params=_params(("parallel", "parallel", "arbitrary")),
        name="moba",
    )(slopes, qkv, qkv, qkv)


def _fox_prep_kernel(f_ref, bf_ref, c_ref, carry_ref):
    ts = f_ref.shape[1]

    @pl.when(pl.program_id(1) == 0)
    def _():
        carry_ref[...] = jnp.zeros_like(carry_ref)

    z = f_ref[0] + bf_ref[...]
    lf = -_softplus(-z)
    tri = (lax.broadcasted_iota(I32, (ts, ts), 0) >= lax.broadcasted_iota(I32, (ts, ts), 1)).astype(F32)
    c = _dot(tri, lf, precision=HI) + carry_ref[0:1, :]
    c_ref[0] = c
    carry_ref[0:1, :] = c[ts - 1:ts, :]


def _fox_prep(f_logit, b_f, ts, cb=0):
    B, S, _ = f_logit.shape
    W = LANES
    return pl.pallas_call(
        _fox_prep_kernel,
        grid=(B, S // ts),
        in_specs=[pl.BlockSpec((1, ts, W), lambda b, j: (b, j, cb)), pl.BlockSpec((1, W), lambda b, j: (0, 0))],
        out_specs=pl.BlockSpec((1, ts, W), lambda b, j: (b, j, 0)),
        out_shape=jax.ShapeDtypeStruct((B, S, W), F32),
        scratch_shapes=[pltpu.VMEM((8, W), F32)],
        compiler_params=_params(("parallel", "arbitrary")),
        name="fox_prep",
    )(f_logit, b_f)


def _fox_kernel(q_ref, k_ref, v_ref, ccol_ref, crow_ref, o_ref):
    tq = q_ref.shape[1]
    i = pl.program_id(2)
    scale = FOX_HEAD_DIM ** -0.5
    q = q_ref[0]
    head_lane = lax.broadcasted_iota(I32, (tq, LANES), 1) == pl.program_id(1)
    cq = jnp.sum(jnp.where(head_lane, ccol_ref[0], 0.0), axis=1, keepdims=True)
    rc = lax.broadcasted_iota(I32, (tq, tq), 0) - lax.broadcasted_iota(I32, (tq, tq), 1)

    def scores(n):
        start = pl.multiple_of(n * tq, tq)
        s = _dot(q, k_ref[0, pl.ds(start, tq), :], _NT) * scale + cq - crow_ref[0, 0, n]
        return s, v_ref[0, pl.ds(start, tq), :]

    s, vblk = scores(i)
    s = jnp.where(rc >= 0, s, -jnp.inf)
    m0 = jnp.max(s, axis=1, keepdims=True)
    p = jnp.exp(s - m0)
    l0 = jnp.sum(p, axis=1, keepdims=True)
    acc0 = _dot(p.astype(BF16), vblk)

    def body(n, carry):
        m, l, acc = carry
        s, vblk = scores(n)
        m_new = jnp.maximum(m, jnp.max(s, axis=1, keepdims=True))
        alpha = jnp.exp(m - m_new)
        p = jnp.exp(s - m_new)
        l = alpha * l + jnp.sum(p, axis=1, keepdims=True)
        acc = alpha * acc + _dot(p.astype(BF16), vblk)
        return m_new, l, acc

    _, l, acc = lax.fori_loop(0, i, body, (m0, l0, acc0))
    o_ref[0] = (acc / l).astype(o_ref.dtype)


def _fox(qkv, c, tq, cb0=0):
    B, S, _ = qkv.shape
    H, d = FOX_HEADS, FOX_HEAD_DIM
    nq = S // tq
    c_row = jnp.transpose(c[:, :, :H], (0, 2, 1)).reshape(B, H, nq, 1, tq)
    return pl.pallas_call(
        _fox_kernel,
        grid=(B, H, nq),
        in_specs=[
            pl.BlockSpec((1, tq, d), lambda b, h, i: (b, i, cb0 + h)),
            pl.BlockSpec((1, S, d), lambda b, h, i: (b, 0, cb0 + H + h)),
            pl.BlockSpec((1, S, d), lambda b, h, i: (b, 0, cb0 + 2 * H + h)),
            pl.BlockSpec((1, tq, LANES), lambda b, h, i: (b, i, 0)),
            pl.BlockSpec((1, 1, nq, 1, tq), lambda b, h, i: (b, h, 0, 0, 0)),
        ],
        out_specs=pl.BlockSpec((1, tq, d), lambda b, h, i: (b, i, h)),
        out_shape=jax.ShapeDtypeStruct((B, S, H * d), BF16),
        compiler_params=_params(("parallel", "parallel", "arbitrary")),
        name="fox",
    )(qkv, qkv, qkv, c, c_row)


def _ret_kernel(q_ref, k_ref, v_ref, g_ref, lng_ref, lnb_ref, o_ref, state_ref):
    C = q_ref.shape[1]
    dk, dv = RET_KEY_DIM, RET_VALUE_DIM

    @pl.when(pl.program_id(1) == 0)
    def _():
        state_ref[...] = jnp.zeros_like(state_ref)

    row = lax.broadcasted_iota(I32, (C, C), 0)
    col = lax.broadcasted_iota(I32, (C, C), 1)
    diff = (row - col).astype(F32)
    tcol = lax.broadcasted_iota(I32, (C, 1), 0).astype(F32)
    scale = dk ** -0.5
    for h in range(RET_HEADS):
        log_gamma = float(np.log(1.0 - 2.0 ** (-5.0 - h)))
        intra = jnp.where(diff >= 0, jnp.exp(log_gamma * jnp.maximum(diff, 0.0)), 0.0)
        q_dec = jnp.exp(log_gamma * (tcol + 1.0))
        k_dec = jnp.exp(log_gamma * (C - 1.0 - tcol))
        blk_dec = float(np.exp(log_gamma * C))
        qh = q_ref[0, :, h * dk:(h + 1) * dk]
        kh = (k_ref[0, :, h * dk:(h + 1) * dk].astype(F32) * scale).astype(BF16)
        vh = v_ref[0, :, h * dv:(h + 1) * dv]
        R = state_ref[h]
        inner = _dot(qh, kh, _NT) * intra
        o = _dot(inner.astype(BF16), vh) + _dot(qh, R.astype(BF16)) * q_dec
        state_ref[h] = R * blk_dec + _dot((kh.astype(F32) * k_dec).astype(BF16), vh, _TN)
        mu = jnp.mean(o, axis=1, keepdims=True)
        d = o - mu
        var = jnp.mean(d * d, axis=1, keepdims=True)
        y = d * lax.rsqrt(var + RET_GN_EPS) * lng_ref[:, h * dv:(h + 1) * dv] + lnb_ref[:, h * dv:(h + 1) * dv]
        g = g_ref[0, :, h * dv:(h + 1) * dv].astype(F32)
        o_ref[0, :, h * dv:(h + 1) * dv] = (g * _sigmoid(g) * y).astype(o_ref.dtype)


def _retention(ret, ln_g, ln_b, C, col0=0):
    B, S, _ = ret.shape
    qk = RET_HEADS * RET_KEY_DIM
    assert col0 % RET_W == 0
    qb, vb = col0 // qk, col0 // RET_W
    return pl.pallas_call(
        _ret_kernel,
        grid=(B, S // C),
        in_specs=[
            pl.BlockSpec((1, C, qk), lambda b, c: (b, c, qb)),
            pl.BlockSpec((1, C, qk), lambda b, c: (b, c, qb + 1)),
            pl.BlockSpec((1, C, RET_W), lambda b, c: (b, c, vb + 1)),
            pl.BlockSpec((1, C, RET_W), lambda b, c: (b, c, vb + 2)),
            pl.BlockSpec((1, RET_W), lambda b, c: (0, 0)),
            pl.BlockSpec((1, RET_W), lambda b, c: (0, 0)),
        ],
        out_specs=pl.BlockSpec((1, C, RET_W), lambda b, c: (b, c, 0)),
        out_shape=jax.ShapeDtypeStruct((B, S, RET_W), BF16),
        scratch_shapes=[pltpu.VMEM((RET_HEADS, RET_KEY_DIM, RET_VALUE_DIM), F32)],
        compiler_params=_params(("parallel", "arbitrary")),
        name="retention",
    )(ret, ret, ret, ret, ln_g, ln_b)


def _rwkv_prep_kernel(x_ref, mu_ref, w0_ref, wup_ref, a0_ref, aup_ref, gup_ref, kk_ref, ka_ref, rk_ref, hsum_ref,
                      r_o, lw_o, k_o, v_o, a_o, b_o, g_o, bonus_o, carry_ref):
    ts = x_ref.shape[1]
    W = RWKV_W

    @pl.when(pl.program_id(1) == 0)
    def _():
        carry_ref[...] = jnp.zeros_like(carry_ref)

    x = x_ref[0]
    row = lax.broadcasted_iota(I32, x.shape, 0)
    shifted = jnp.where(row == 0, carry_ref[0:1, :], pltpu.roll(x, 1, 0))
    carry_ref[0:1, :] = x[ts - 1:ts, :]
    xs = x + (shifted - x) * mu_ref[...]
    r = xs[:, 0:W]
    k = xs[:, W:2 * W]
    v = xs[:, 2 * W:3 * W]
    wd = xs[:, 3 * W:3 * W + LANES]
    ad = xs[:, 3 * W + LANES:3 * W + 2 * LANES]
    gd = xs[:, 3 * W + 2 * LANES:3 * W + 3 * LANES]
    w_log = -_softplus(-(w0_ref[...] + _dot(jnp.tanh(wd), wup_ref[...], precision=HI))) - 0.5
    a = _sigmoid(a0_ref[...] + _dot(ad, aup_ref[...], precision=HI))
    g = _dot(_sigmoid(gd), gup_ref[...], precision=HI)
    kk = k * kk_ref[...]
    norm = jnp.sqrt(_dot(kk * kk, hsum_ref[...], precision=HI))
    kk = kk / jnp.maximum(norm, 1e-12)
    k2 = k * (1.0 + (a - 1.0) * ka_ref[...])
    r_o[0] = r
    lw_o[0] = -jnp.exp(w_log)
    k_o[0] = k2
    v_o[0] = v
    a_o[0] = -kk
    b_o[0] = kk * a
    g_o[0] = g
    bonus_o[0] = _dot(r * k2 * rk_ref[...], hsum_ref[...], precision=HI) * v


def _rwkv_prep(cols, mu, w0, wup, a0, aup, gup, k_k, k_a, r_k, hsum, ts):
    B, S, Wc = cols.shape
    W = RWKV_W
    vec = lambda n: pl.BlockSpec((1, n), lambda b, j: (0, 0))
    mat = lambda r, c: pl.BlockSpec((r, c), lambda b, j: (0, 0))
    out = jax.ShapeDtypeStruct((B, S, W), F32)
    return pl.pallas_call(
        _rwkv_prep_kernel,
        grid=(B, S // ts),
        in_specs=[pl.BlockSpec((1, ts, Wc), lambda b, j: (b, j, 0)), vec(Wc), vec(W), mat(LANES, W), vec(W),
                  mat(LANES, W), mat(LANES, W), vec(W), vec(W), vec(W), mat(W, W)],
        out_specs=[pl.BlockSpec((1, ts, W), lambda b, j: (b, j, 0))] * 8,
        out_shape=[out] * 8,
        scratch_shapes=[pltpu.VMEM((8, Wc), F32)],
        compiler_params=_params(("parallel", "arbitrary"), V7X_VMEM_LIMIT),
        name="rwkv_prep",
    )(cols, mu, w0, wup, a0, aup, gup, k_k, k_a, r_k, hsum)


def _rwkv_scan_kernel(r_ref, lw_ref, k_ref, v_ref, a_ref, b_ref, g_ref, bonus_ref, lng_ref, lnb_ref, hmean_ref,
                      o_ref, state_ref, y_ref):
    C = RWKV_CHUNK
    N = RWKV_HEAD_DIM
    TS = r_ref.shape[1]

    @pl.when(pl.program_id(1) == 0)
    def _():
        state_ref[...] = jnp.zeros_like(state_ref)

    row = lax.broadcasted_iota(I32, (C, C), 0)
    col = lax.broadcasted_iota(I32, (C, C), 1)
    incl = row >= col
    strict = row > col
    eye = (row == col).astype(F32)
    tri = incl.astype(F32)

    def chunk(c, carry):
        sl = pl.ds(pl.multiple_of(c * C, C), C)
        lw = lw_ref[0, sl, :]
        cum = _dot(tri, lw, precision=HI)
        cum_last = cum[C - 1:C, :]
        e_pos = jnp.exp(cum)
        e_neg = jnp.exp(-cum)
        e_last = jnp.exp(cum_last - cum)
        rt = r_ref[0, sl, :] * e_pos
        at = a_ref[0, sl, :] * jnp.exp(cum - lw)
        b = b_ref[0, sl, :]
        k = k_ref[0, sl, :]
        v = v_ref[0, sl, :]
        bt = b * e_neg
        kt = k * e_neg
        bl = b * e_last
        kl = k * e_last
        pc = jnp.exp(cum_last)
        heads = range(RWKV_HEADS)
        sl_h = [slice(h * N, (h + 1) * N) for h in heads]
        at_h = [at[:, s].astype(BF16) for s in sl_h]
        rt_h = [rt[:, s] for s in sl_h]
        v_h = [v[:, s].astype(BF16) for s in sl_h]
        bt_h = [bt[:, s].astype(BF16) for s in sl_h]
        kt_h = [kt[:, s].astype(BF16) for s in sl_h]
        bl_h = [bl[:, s].astype(BF16) for s in sl_h]
        kl_h = [kl[:, s].astype(BF16) for s in sl_h]
        lhs2 = [jnp.concatenate([at_h[h], rt_h[h].astype(BF16)], axis=0) for h in heads]
        sb = [_dot(lhs2[h], bt_h[h], _NT) for h in heads]
        sk = [_dot(lhs2[h], kt_h[h], _NT) for h in heads]
        a_ab = [jnp.where(strict, sb[h][:C], 0.0) for h in heads]
        m_rb = [jnp.where(incl, sb[h][C:], 0.0).astype(BF16) for h in heads]
        a_ak = [jnp.where(strict, sk[h][:C], 0.0).astype(BF16) for h in heads]
        m_rk = [jnp.where(incl, sk[h][C:], 0.0).astype(BF16) for h in heads]
        xb = [a_ab[h].astype(BF16) for h in heads]
        t_inv = [eye + a_ab[h] for h in heads]
        akv = [_dot(a_ak[h], v_h[h]).astype(BF16) for h in heads]
        mrkv = [_dot(m_rk[h], v_h[h]) for h in heads]
        klv = [_dot(kl_h[h], v_h[h], _TN) for h in heads]
        x = [_dot(xb[h], xb[h]) for h in heads]
        for step in range(int(np.log2(C)) - 1):
            xb = [x[h].astype(BF16) for h in heads]
            if step < int(np.log2(C)) - 2:
                x = [_dot(xb[h], xb[h]) for h in heads]
            t_inv = [t_inv[h] + _dot(t_inv[h].astype(BF16), xb[h]) for h in heads]
        t_b = [t_inv[h].astype(BF16) for h in heads]
        abar = [_dot(t_b[h], at_h[h]).astype(BF16) for h in heads]
        u0 = [_dot(t_b[h], akv[h]).astype(BF16) for h in heads]
        rbar = [(rt_h[h] + _dot(m_rb[h], abar[h])).astype(BF16) for h in heads]
        y0 = [_dot(m_rb[h], u0[h]) + mrkv[h] for h in heads]
        gmat = [(eye * pc[:, sl_h[h]] + _dot(bl_h[h], abar[h], _TN)).astype(BF16) for h in heads]
        hadd = [_dot(bl_h[h], u0[h], _TN) + klv[h] for h in heads]
        h0b = [state_ref[h].astype(BF16) for h in heads]
        for h in heads:
            y_ref[:, sl_h[h]] = _dot(rbar[h], h0b[h]) + y0[h]
        for h in heads:
            state_ref[h] = _dot(gmat[h], h0b[h]) + hadd[h]
        y = y_ref[...]
        mu = _dot(y, hmean_ref[...], precision=HI)
        d = y - mu
        var = _dot(d * d, hmean_ref[...], precision=HI)
        yn = d * lax.rsqrt(var + RWKV_GN_EPS) * lng_ref[...] + lnb_ref[...]
        o_ref[0, sl, :] = ((yn + bonus_ref[0, sl, :]) * g_ref[0, sl, :]).astype(o_ref.dtype)
        return carry

    lax.fori_loop(0, TS // C, chunk, 0)


def _rwkv_scan(r, lw, k, v, a, b, g, bonus, ln_g, ln_b, hmean, ts):
    B, S, W = r.shape
    blk = pl.BlockSpec((1, ts, W), lambda bb, j: (bb, j, 0))
    vec = pl.BlockSpec((1, W), lambda bb, j: (0, 0))
    return pl.pallas_call(
        _rwkv_scan_kernel,
        grid=(B, S // ts),
        in_specs=[blk] * 8 + [vec, vec, pl.BlockSpec((W, W), lambda bb, j: (0, 0))],
        out_specs=blk,
        out_shape=jax.ShapeDtypeStruct((B, S, W), BF16),
        scratch_shapes=[pltpu.VMEM((RWKV_HEADS, RWKV_HEAD_DIM, RWKV_HEAD_DIM), F32),
                        pltpu.VMEM((RWKV_CHUNK, W), F32)],
        compiler_params=_params(("parallel", "arbitrary"), V7X_VMEM_LIMIT),
        name="rwkv_scan",
    )(r, lw, k, v, a, b, g, bonus, ln_g, ln_b, hmean)


def _layer_norm_rows(z, g, b):
    mu = jnp.mean(z, axis=1, keepdims=True)
    d = z - mu
    var = jnp.mean(d * d, axis=1, keepdims=True)
    return d * lax.rsqrt(var + LN_EPS) * g + b


def _pack_bf16_pairs(lo, hi):
    bits = lambda a: pltpu.bitcast(a.astype(BF16).astype(F32), U32)
    return (bits(lo) >> 16) | (bits(hi) & jnp.uint32(0xFFFF0000))


def _outproj_kernel(ya_ref, yb_ref, yc_ref, yd_ref, h_ref, w_ref, g_ref, b_ref, rw_ref, rb_ref,
                    h1_ref, hp_ref, ti_ref, tg_ref):
    mix = _dot(ya_ref[...], w_ref[0]) + _dot(yb_ref[...], w_ref[1]) + _dot(yc_ref[...], w_ref[2]) + _dot(yd_ref[...], w_ref[3])
    hn = _layer_norm_rows(DEEPNORM_ALPHA * h_ref[...] + mix, g_ref[...], b_ref[...])
    h1_ref[...] = hn
    hp_ref[...] = _pack_bf16_pairs(hn[:, :hn.shape[1] // 2], hn[:, hn.shape[1] // 2:])
    tm = hn.shape[0]
    lane = lax.broadcasted_iota(I32, (tm, LANES), 1)
    lane_f = lane.astype(F32)
    hn_hi = hn.astype(BF16)
    hn_lo = (hn - hn_hi.astype(F32)).astype(BF16)
    both = _dot(hn_hi, rw_ref[...])
    logits = both[:, :LANES] + (both[:, LANES:] + _dot(hn_lo, rw_ref[:, :LANES])) + rb_ref[...]
    logits = jnp.where(lane < N_EXPERTS, logits, -jnp.inf)
    top = _topk_lanes(logits, lane_f, TOP_K)
    v0 = top[0][0]
    es = [jnp.exp(vk - v0) for vk, _ in top]
    denom = es[0] + es[1] + es[2] + es[3]
    ti = jnp.zeros((tm, LANES), F32)
    tg = jnp.zeros((tm, LANES), F32)
    for kk in range(TOP_K):
        ti = jnp.where(lane == kk, top[kk][1], ti)
        tg = jnp.where(lane == kk, es[kk] / denom, tg)
    ti_ref[...] = ti.astype(I32)
    tg_ref[...] = tg


def _outproj_ln_router(ys, h, w_out4, ln_g, ln_b, rw, rb, tm):
    T, D = h.shape
    Wg = ys[0].shape[1]
    yspec = pl.BlockSpec((tm, Wg), lambda i: (i, 0))
    row = pl.BlockSpec((tm, D), lambda i: (i, 0))
    vec = lambda n: pl.BlockSpec((1, n), lambda i: (0, 0))
    small = pl.BlockSpec((tm, LANES), lambda i: (i, 0))
    return pl.pallas_call(
        _outproj_kernel,
        grid=(T // tm,),
        in_specs=[yspec] * 4 + [row, pl.BlockSpec((4, Wg, D), lambda i: (0, 0, 0)), vec(D), vec(D),
                                pl.BlockSpec((D, 2 * LANES), lambda i: (0, 0)), vec(LANES)],
        out_specs=[row, pl.BlockSpec((tm, D // 2), lambda i: (i, 0)), small, small],
        out_shape=[jax.ShapeDtypeStruct((T, D), F32), jax.ShapeDtypeStruct((T, D // 2), U32),
                   jax.ShapeDtypeStruct((T, LANES), I32), jax.ShapeDtypeStruct((T, LANES), F32)],
        compiler_params=_params(("parallel",), V7X_VMEM_LIMIT),
        name="outproj_ln_router",
    )(*ys, h, w_out4, ln_g, ln_b, rw, rb)


def _unpack_bf16_pairs(w):
    return pltpu.bitcast(w << 16, F32), pltpu.bitcast(w & jnp.uint32(0xFFFF0000), F32)


def _moe_kernel(te_ref, nu_ref, src0_ref, srcn_ref, dstp_ref, hp_hbm, w1g_ref, w1l_ref, b1g_ref, b1l_ref, w2a_ref,
                w2b_ref, b2a_ref, b2b_ref, y_hbm, xu_ref, xb_ref, act_ref, acc_ref, gsem, ssem, *, n_real_rows):
    i = pl.program_id(0)
    j = pl.program_id(1)
    ns = MOE_STEPS
    nj = 2 * ns
    tm, half = xu_ref.shape
    rows_per_step = tm // nj
    th = act_ref.shape[1] // ns
    tn = half // ns
    nu = nu_ref[0]
    valid = i < nu
    slot = i % 2
    base = j * rows_per_step

    def wait_gather():
        pltpu.make_async_copy(hp_hbm.at[pl.ds(0, tm), :], xu_ref, gsem).wait()

    def wait_scatter():
        pltpu.make_async_copy(acc_ref.at[0], y_hbm.at[pl.ds(0, tm), :], ssem).wait()

    def issue_gather_chunk():
        for r in range(rows_per_step):
            t = srcn_ref[0, 0, base + r]
            pltpu.make_async_copy(hp_hbm.at[pl.ds(t, 1), :], xu_ref.at[pl.ds(base + r, 1), :], gsem).start()

    def issue_scatter_chunk():
        for r in range(rows_per_step):
            t = dstp_ref[0, 0, base + r]
            pltpu.make_async_copy(acc_ref.at[1 - slot, pl.ds(base + r, 1), :], y_hbm.at[pl.ds(t, 1), :], ssem).start()

    @pl.when((i == 0) & (j == 0))
    def _():
        acc_ref[1] = jnp.zeros(acc_ref.shape[1:], U32)
        fills = [pltpu.make_async_copy(acc_ref.at[1], y_hbm.at[pl.ds(n_real_rows + f * tm, tm), :], ssem)
                 for f in range((y_hbm.shape[0] - n_real_rows) // tm)]
        for fill in fills:
            fill.start()
        for fill in fills:
            fill.wait()

        def issue(r, carry):
            t = src0_ref[0, 0, r]
            pltpu.make_async_copy(hp_hbm.at[pl.ds(t, 1), :], xu_ref.at[pl.ds(r, 1), :], gsem).start()
            return carry

        lax.fori_loop(0, tm, issue, 0)

    @pl.when((j == 0) & (i <= nu))
    def _():
        wait_gather()

    @pl.when(valid & (j == 0))
    def _():
        lo, hi = _unpack_bf16_pairs(xu_ref[...])
        xb_ref[:, :half] = lo.astype(BF16)
        xb_ref[:, half:] = hi.astype(BF16)

    @pl.when(valid & (j < ns))
    def _():
        issue_gather_chunk()
        issue_scatter_chunk()
        xb = xb_ref[...]
        w1g = w1g_ref[0].astype(BF16)
        w1l = w1l_ref[0].astype(BF16)
        b1g = b1g_ref[0]
        b1l = b1l_ref[0]
        acts = []
        for c in range(2):
            cs = slice(c * (th // 2), (c + 1) * (th // 2))
            glu = jnp.minimum(_dot(xb, w1g[:, cs]) + b1g[:, cs], SWIGLU_LIMIT)
            lin = jnp.clip(_dot(xb, w1l[:, cs]) + b1l[:, cs], -SWIGLU_LIMIT, SWIGLU_LIMIT)
            acts.append((glu * _sigmoid(SWIGLU_ALPHA * glu) * (lin + 1.0)).astype(BF16))
        act = jnp.concatenate(acts, axis=1)
        for jj in range(ns):
            @pl.when(j == jj)
            def _():
                act_ref[:, jj * th:(jj + 1) * th] = act

    @pl.when(valid & (j >= ns))
    def _():
        act = act_ref[...]
        lo = _dot(act, w2a_ref[0].astype(BF16)) + b2a_ref[0]
        issue_gather_chunk()
        hi = _dot(act, w2b_ref[0].astype(BF16)) + b2b_ref[0]
        issue_scatter_chunk()
        packed = _pack_bf16_pairs(lo, hi)
        for nn in range(ns):
            @pl.when(j == ns + nn)
            def _():
                acc_ref[slot, :, nn * tn:(nn + 1) * tn] = packed

        @pl.when(j == nj - 1)
        def _():
            wait_scatter()

    @pl.when(i == nu)
    def _():
        issue_scatter_chunk()

        @pl.when(j == nj - 1)
        def _():
            wait_scatter()


def _moe_ffn(hp, src_tok, dst_row, tile_expert, n_used, w1, b1, w2, b2, tm):
    NT = src_tok.shape[0]
    _, D, _ = w1.shape
    ns = MOE_STEPS
    nj = 2 * ns
    half = D // 2
    th = D_EXPERT // ns
    tn = half // ns
    assert tm % nj == 0

    def step(i, j, nu):
        return jnp.where(i < nu[0], j, nj - 1)

    def j1(i, j, te, nu):
        return jnp.minimum(step(i, j, nu), ns - 1)

    def j2(i, j, te, nu):
        return jnp.maximum(step(i, j, nu) - ns, 0)

    smem = lambda f: pl.BlockSpec((1, 1, tm), f, memory_space=pltpu.SMEM)
    assert (NT * tm - hp.shape[0] * TOP_K) % tm == 0
    return pl.pallas_call(
        functools.partial(_moe_kernel, n_real_rows=hp.shape[0] * TOP_K),
        grid_spec=pltpu.PrefetchScalarGridSpec(
            num_scalar_prefetch=2,
            grid=(NT, nj),
            in_specs=[
                smem(lambda i, j, te, nu: (0, 0, 0)),
                smem(lambda i, j, te, nu: (jnp.minimum(i + 1, NT - 1), 0, 0)),
                smem(lambda i, j, te, nu: (jnp.maximum(i - 1, 0), 0, 0)),
                pl.BlockSpec(memory_space=pl.ANY),
                pl.BlockSpec((1, D, th), lambda i, j, te, nu: (te[i], 0, j1(i, j, te, nu))),
                pl.BlockSpec((1, D, th), lambda i, j, te, nu: (te[i], 0, ns + j1(i, j, te, nu))),
                pl.BlockSpec((1, 1, th), lambda i, j, te, nu: (te[i], 0, j1(i, j, te, nu))),
                pl.BlockSpec((1, 1, th), lambda i, j, te, nu: (te[i], 0, ns + j1(i, j, te, nu))),
                pl.BlockSpec((1, D_EXPERT, tn), lambda i, j, te, nu: (te[i], 0, j2(i, j, te, nu))),
                pl.BlockSpec((1, D_EXPERT, tn), lambda i, j, te, nu: (te[i], 0, ns + j2(i, j, te, nu))),
                pl.BlockSpec((1, 1, tn), lambda i, j, te, nu: (te[i], 0, j2(i, j, te, nu))),
                pl.BlockSpec((1, 1, tn), lambda i, j, te, nu: (te[i], 0, ns + j2(i, j, te, nu))),
            ],
            out_specs=pl.BlockSpec(memory_space=pl.ANY),
            scratch_shapes=[pltpu.VMEM((tm, half), U32), pltpu.VMEM((tm, D), BF16), pltpu.VMEM((tm, D_EXPERT), BF16),
                            pltpu.VMEM((2, tm, half), U32), pltpu.SemaphoreType.DMA, pltpu.SemaphoreType.DMA],
        ),
        out_shape=jax.ShapeDtypeStruct((NT * tm, half), U32),
        compiler_params=_params(("arbitrary", "arbitrary"), V7X_VMEM_LIMIT_MOE),
        name="moe_ffn",
    )(tile_expert, n_used, src_tok, src_tok, dst_row, hp, w1, w1, b1, b1, w2, w2, b2, b2)


def _combine_kernel(y0_ref, y1_ref, y2_ref, y3_ref, tg_ref, h_ref, g_ref, b_ref, o_ref):
    tg = tg_ref[...]
    lo = hi = None
    for kk, y_ref in enumerate((y0_ref, y1_ref, y2_ref, y3_ref)):
        ylo, yhi = _unpack_bf16_pairs(y_ref[...])
        gk = tg[:, kk:kk + 1]
        lo = gk * ylo if lo is None else lo + gk * ylo
        hi = gk * yhi if hi is None else hi + gk * yhi
    ffn = jnp.concatenate([lo, hi], axis=1)
    o_ref[...] = _layer_norm_rows(DEEPNORM_ALPHA * h_ref[...] + ffn, g_ref[...], b_ref[...])


def _combine_ln(y, tg, h1, ln_g, ln_b, tq):
    T, D = h1.shape
    row = pl.BlockSpec((tq, D), lambda i: (i, 0))
    vec = pl.BlockSpec((1, D), lambda i: (0, 0))
    nq = T // tq
    ys = [pl.BlockSpec((tq, D // 2), functools.partial(lambda i, kk: (kk * nq + i, 0), kk=kk)) for kk in range(TOP_K)]
    return pl.pallas_call(
        _combine_kernel,
        grid=(nq,),
        in_specs=ys + [pl.BlockSpec((tq, LANES), lambda i: (i, 0)), row, vec, vec],
        out_specs=row,
        out_shape=jax.ShapeDtypeStruct((T, D), F32),
        compiler_params=_params(("parallel",), V7X_VMEM_LIMIT),
        name="combine_ln",
    )(y, y, y, y, tg, h1, ln_g, ln_b)


def _dispatch_tables(top_idx, tm, n_tiles):
    T = top_idx.shape[0]
    P = T * TOP_K
    E = N_EXPERTS
    R = n_tiles * tm
    shift = int(np.ceil(np.log2(R)))
    assert (E + 1) << shift < 2 ** 31
    e = top_idx.reshape(P)
    experts = jnp.arange(E, dtype=I32)
    counts = jnp.sum((e[:, None] == experts[None, :]).astype(I32), axis=0)
    tiles = (counts + tm - 1) // tm
    tile_end = jnp.cumsum(tiles)
    n_used = tile_end[-1]
    pad_end = jnp.cumsum(tiles * tm - counts)
    d = jnp.arange(R - P, dtype=I32)
    pad_expert = jnp.sum((pad_end[None, :] <= d[:, None]).astype(I32), axis=1)
    keys = jnp.concatenate([(e << shift) + jnp.arange(P, dtype=I32), (pad_expert << shift) + P + d])
    ids = jnp.sort(keys) & ((1 << shift) - 1)
    real = ids < P
    src_tok = jnp.where(real, ids // TOP_K, 0).astype(I32)
    dst_row = jnp.where(real, (ids % TOP_K) * T + ids // TOP_K, ids).astype(I32)
    tile_ids = jnp.arange(n_tiles, dtype=I32)
    last = jnp.maximum(n_used - 1, 0)
    te = jnp.sum((tile_end[None, :] <= jnp.minimum(tile_ids, last)[:, None]).astype(I32), axis=1)
    return src_tok, dst_row, te.astype(I32), n_used.reshape(1).astype(I32)


def _pad_cols(a, n):
    return jnp.pad(a, ((0, 0), (0, n - a.shape[1])))


def _pad_rows(a, n):
    return jnp.pad(a, ((0, n - a.shape[0]), (0, 0)))


def _rwkv_col_layout(a, tail=None):
    W = RWKV_W
    rkv, wd, ad, gd = a[:, :3 * W], a[:, 3 * W:3 * W + 64], a[:, 3 * W + 64:3 * W + 128], a[:, 3 * W + 128:]
    if tail is None:
        tail = jnp.zeros((a.shape[0], LANES), a.dtype)
    return jnp.concatenate([rkv, _pad_cols(wd, LANES), _pad_cols(ad, LANES), gd, tail], axis=1)


def _layer(layer, h, B, S, w_in, mu, w0, w_up, a0, a_up, g_up, k_k, k_a, r_k, rln_g, rln_b, fox_b_f, ret_ln_g, ret_ln_b,
           w_out, ln1_g, ln1_b, router_w, router_b, w1, b1, w2, b2, ln2_g, ln2_b):
    T, D = h.shape
    w_att, w_f32 = _prep_in_weights(w_in, layer)
    fox_cb0 = MOBA_COLS // FOX_HEAD_DIM
    ret_col0 = MOBA_COLS + 3 * FOX_W
    foxf_cb = w_f32.shape[1] // LANES - 1

    tm = min(512, T)
    att = _matmul(h, w_att, BF16, tm, MOBA_COLS).reshape(B, S, -1)
    rwkv_cols = _matmul(h, w_f32, F32, tm, 1024).reshape(B, S, -1)

    slopes = jnp.exp2(-8.0 * jnp.arange(1, MOBA_HEADS + 1, dtype=F32) / MOBA_HEADS)
    y_a = _moba(att, slopes, min(512, S))

    head_of = np.arange(RWKV_W) // RWKV_HEAD_DIM
    same = (head_of[:, None] == head_of[None, :]).astype(np.float32)
    hsum = jnp.asarray(same)
    hmean = jnp.asarray(same / RWKV_HEAD_DIM)
    row = lambda a: a.reshape(1, -1)
    ts_r = min(256, S)
    prep = _rwkv_prep(rwkv_cols, _rwkv_col_layout(row(mu)), row(w0), _pad_rows(w_up, LANES), row(a0),
                      _pad_rows(a_up, LANES), g_up, row(k_k), row(k_a), row(r_k), hsum, ts_r)
    y_b = _rwkv_scan(*prep, row(rln_g), row(rln_b), hmean, min(512, S))

    c = _fox_prep(rwkv_cols, _pad_cols(row(fox_b_f), LANES), min(512, S), foxf_cb)
    y_c = _fox(att, c, min(512, S), fox_cb0)

    y_d = _retention(att, row(ret_ln_g), row(ret_ln_b), min(256, S), ret_col0)

    Wg = D // 4
    ys = [y.reshape(T, Wg) for y in (y_a, y_b, y_c, y_d)]
    rw32 = _pad_cols(router_w, LANES)
    rw_hi = rw32.astype(BF16)
    rw = jnp.concatenate([rw_hi, (rw32 - rw_hi.astype(F32)).astype(BF16)], axis=1)
    rb = _pad_cols(row(router_b), LANES)
    h1, hp, top_i, top_g = _outproj_ln_router(ys, h, w_out.astype(BF16).reshape(4, Wg, D), row(ln1_g), row(ln1_b),
                                              rw, rb, min(256, T))

    tm_e = min(MOE_TILE_ROWS, T)
    n_tiles = (T * TOP_K) // tm_e + N_EXPERTS + 1
    src_tok, dst_row, te, n_used = _dispatch_tables(top_i[:, :TOP_K], tm_e, n_tiles)
    y = _moe_ffn(hp, src_tok.reshape(n_tiles, 1, tm_e), dst_row.reshape(n_tiles, 1, tm_e), te + layer * N_EXPERTS,
                 n_used, w1, b1, w2, b2, tm_e)
    return _combine_ln(y, top_g, h1, row(ln2_g), row(ln2_b), min(256, T))


def kernel(x, w_in, rwkv_mu, rwkv_w0, rwkv_w_up, rwkv_a0, rwkv_a_up, rwkv_g_up, rwkv_k_k, rwkv_k_a, rwkv_r_k, rwkv_ln_g, rwkv_ln_b, fox_b_f, ret_ln_g, ret_ln_b, w_out, ln1_g, ln1_b, router_w, router_b, exp_w1, exp_b1, exp_w2, exp_b2, ln2_g, ln2_b):
    B, S, D = x.shape
    h = x.reshape(B * S, D)
    LE = exp_w1.shape[0] * exp_w1.shape[1]
    w1 = exp_w1.reshape(LE, D, -1)
    b1 = exp_b1.reshape(LE, 1, -1)
    w2 = exp_w2.reshape(LE, -1, D)
    b2 = exp_b2.reshape(LE, 1, D)
    for l in range(DEPTH):
        h = _layer(l, h, B, S, w_in, rwkv_mu[l], rwkv_w0[l], rwkv_w_up[l], rwkv_a0[l], rwkv_a_up[l], rwkv_g_up[l],
                   rwkv_k_k[l], rwkv_k_a[l], rwkv_r_k[l].reshape(-1), rwkv_ln_g[l], rwkv_ln_b[l], fox_b_f[l],
                   ret_ln_g[l], ret_ln_b[l], w_out[l], ln1_g[l], ln1_b[l], router_w[l], router_b[l],
                   w1, b1, w2, b2, ln2_g[l], ln2_b[l])
    return h.reshape(B, S, D)
```

```python
import functools

import numpy as np
import jax
import jax.numpy as jnp
from jax import lax
from jax.experimental import pallas as pl
from jax.experimental.pallas import tpu as pltpu

F32 = jnp.float32
BF16 = jnp.bfloat16
I32 = jnp.int32
U32 = jnp.uint32
HI = lax.Precision.HIGHEST

D_MODEL = 2048
DEPTH = 2
MOBA_HEADS = 4
MOBA_HEAD_DIM = 128
MOBA_W = MOBA_HEADS * MOBA_HEAD_DIM
MOBA_BLOCK = 256
MOBA_TOPK = 3
RWKV_HEADS = 8
RWKV_HEAD_DIM = 64
RWKV_W = RWKV_HEADS * RWKV_HEAD_DIM
RWKV_W_LORA = 64
RWKV_A_LORA = 64
RWKV_G_LORA = 128
RWKV_GN_EPS = 64e-5
RWKV_CHUNK = 64
FOX_HEADS = 4
FOX_HEAD_DIM = 128
FOX_W = FOX_HEADS * FOX_HEAD_DIM
RET_HEADS = 4
RET_KEY_DIM = 64
RET_VALUE_DIM = 128
RET_W = RET_HEADS * RET_VALUE_DIM
RET_GN_EPS = 1e-5
MOBA_COLS = 3 * MOBA_W
RWKV_SHIFT_COLS = 3 * RWKV_W + RWKV_W_LORA + RWKV_A_LORA + RWKV_G_LORA
FOX_COLS = 3 * FOX_W + FOX_HEADS
RET_COLS = 2 * RET_HEADS * RET_KEY_DIM + 2 * RET_W
N_EXPERTS = 32
TOP_K = 4
D_EXPERT = D_MODEL
SWIGLU_ALPHA = 1.702
SWIGLU_LIMIT = 7.0
LN_EPS = 1e-5
DEEPNORM_ALPHA = (2 * DEPTH) ** 0.25

LANES = 128
V7X_VMEM_LIMIT = 56 * 1024 * 1024
V7X_VMEM_LIMIT_MOE = 60 * 1024 * 1024
MOE_TILE_ROWS = 1024
MOE_STEPS = 4

_NT = (((1,), (1,)), ((), ()))
_TN = (((0,), (0,)), ((), ()))


def _dot(a, b, dims=None, precision=None):
    if dims is None:
        dims = (((a.ndim - 1,), (0,)), ((), ()))
    return lax.dot_general(a, b, dims, precision=precision, preferred_element_type=F32)


def _split3(x):
    hi = x.astype(BF16)
    r = x - hi.astype(F32)
    mid = r.astype(BF16)
    return hi, mid, (r - mid.astype(F32)).astype(BF16)


def _dot_exact_rhs(x, m_bf16):
    hi, mid, lo = _split3(x)
    return _dot(hi, m_bf16) + (_dot(mid, m_bf16) + _dot(lo, m_bf16))


def _dot_exact_lhs(m_bf16, x):
    hi, mid, lo = _split3(x)
    return _dot(m_bf16, hi) + (_dot(m_bf16, mid) + _dot(m_bf16, lo))


def _sigmoid(x):
    return 1.0 / (1.0 + jnp.exp(-x))


def _softplus(x):
    return jnp.maximum(x, 0.0) + jnp.log(1.0 + jnp.exp(-jnp.abs(x)))


def _params(sem, vmem=None):
    return pltpu.CompilerParams(dimension_semantics=sem, vmem_limit_bytes=vmem)


def _mm_kernel(x_ref, w_ref, o_ref, xb_ref):
    @pl.when(pl.program_id(1) == 0)
    def _():
        xb_ref[...] = x_ref[...].astype(BF16)

    o_ref[...] = _dot(xb_ref[...], w_ref[...]).astype(o_ref.dtype)


def _win_prep_kernel(w_ref, att_ref, f32g_ref):
    x = w_ref[0]
    rows = x.shape[0]
    W = RWKV_W
    o0 = MOBA_COLS
    o1 = o0 + RWKV_SHIFT_COLS
    o2 = o1 + FOX_COLS
    zeros = lambda n: jnp.zeros((rows, n), F32)
    att = jnp.concatenate([x[:, :o0], x[:, o1:o1 + 3 * FOX_W], x[:, o2:o2 + RET_COLS]], axis=1)
    wd = x[:, o0 + 3 * W:o0 + 3 * W + RWKV_W_LORA]
    ad = x[:, o0 + 3 * W + RWKV_W_LORA:o0 + 3 * W + RWKV_W_LORA + RWKV_A_LORA]
    gd = x[:, o0 + 3 * W + RWKV_W_LORA + RWKV_A_LORA:o1]
    ff = x[:, o1 + 3 * FOX_W:o2]
    f32g = jnp.concatenate([x[:, o0:o0 + 3 * W], wd, zeros(LANES - RWKV_W_LORA), ad, zeros(LANES - RWKV_A_LORA), gd,
                            ff, zeros(LANES - FOX_HEADS)], axis=1)
    att_ref[...] = att.astype(BF16)
    f32g_ref[...] = f32g.astype(BF16)


def _prep_in_weights(w_in, layer, tk=256):
    _, D, C = w_in.shape
    n_att = MOBA_COLS + 3 * FOX_W + RET_COLS
    n_f32 = 3 * RWKV_W + 4 * LANES
    return pl.pallas_call(
        _win_prep_kernel,
        grid=(D // tk,),
        in_specs=[pl.BlockSpec((1, tk, C), lambda i: (layer, i, 0))],
        out_specs=[pl.BlockSpec((tk, n_att), lambda i: (i, 0)), pl.BlockSpec((tk, n_f32), lambda i: (i, 0))],
        out_shape=[jax.ShapeDtypeStruct((D, n_att), BF16), jax.ShapeDtypeStruct((D, n_f32), BF16)],
        compiler_params=_params(("parallel",), V7X_VMEM_LIMIT),
        name="win_prep",
    )(w_in)


def _matmul(x, w, out_dtype, tm, tn):
    T, K = x.shape
    N = w.shape[1]
    assert T % tm == 0 and N % tn == 0
    return pl.pallas_call(
        _mm_kernel,
        grid=(T // tm, N // tn),
        in_specs=[pl.BlockSpec((tm, K), lambda i, j: (i, 0)), pl.BlockSpec((K, tn), lambda i, j: (0, j))],
        out_specs=pl.BlockSpec((tm, tn), lambda i, j: (i, j)),
        out_shape=jax.ShapeDtypeStruct((T, N), out_dtype),
        scratch_shapes=[pltpu.VMEM((tm, K), BF16)],
        compiler_params=_params(("parallel", "arbitrary"), V7X_VMEM_LIMIT),
        name="in_proj",
    )(x, w)


def _topk_lanes(vals, lane_f, k):
    outs = []
    g = vals
    for _ in range(k):
        m = jnp.max(g, axis=1, keepdims=True)
        cand = jnp.where((g == m) & (g > -jnp.inf), lane_f, float(LANES))
        idx = jnp.min(cand, axis=1, keepdims=True)
        outs.append((m, idx))
        g = jnp.where(lane_f == idx, -jnp.inf, g)
    return outs


def _moba_kernel(slopes_ref, q_ref, k_ref, v_ref, o_ref, kmean_ref):
    L = MOBA_BLOCK
    S = k_ref.shape[1]
    nb = S // L
    h = pl.program_id(1)
    i = pl.program_id(2)
    scale = MOBA_HEAD_DIM ** -0.5

    @pl.when(i == 0)
    def _():
        kmean_ref[...] = jnp.zeros_like(kmean_ref)
        for n in range(nb):
            kb = k_ref[0, n * L:(n + 1) * L, :].astype(F32)
            kmean_ref[n:n + 1, :] = jnp.sum(kb, axis=0, keepdims=True) * (1.0 / L)

    TQ = q_ref.shape[1]
    G = TQ // L
    q = q_ref[0]
    lane = lax.broadcasted_iota(I32, (TQ, LANES), 1)
    lane_f = lane.astype(F32)
    log2_l = int(np.log2(L))
    row_sub = lax.broadcasted_iota(I32, (TQ, 1), 0) >> log2_l
    gate = _dot(q.astype(F32), kmean_ref[...], _NT, precision=HI)
    gate = jnp.where(lane < G * i + row_sub, gate, -jnp.inf)
    sel = jnp.zeros((TQ, LANES), F32)
    for _, idx in _topk_lanes(gate, lane_f, MOBA_TOPK):
        sel = jnp.where(lane_f == idx, 1.0, sel)

    slope = slopes_ref[h]
    r_i = lax.broadcasted_iota(I32, (TQ, TQ), 0)
    c_i = lax.broadcasted_iota(I32, (TQ, TQ), 1)
    bias0 = (-slope) * (r_i - c_i).astype(F32)
    col_sub = c_i >> log2_l

    def col_penalty(pens):
        out = pens[G - 1]
        for g in range(G - 2, -1, -1):
            out = jnp.where(col_sub == g, pens[g], out)
        return out

    def sel_pen(blk):
        picked = jnp.max(jnp.where(lane == blk, sel, 0.0), axis=1, keepdims=True)
        return jnp.where(picked > 0.0, 0.0, -jnp.inf)

    def tile(start, pens):
        s = _dot(q, k_ref[0, pl.ds(start, TQ), :], _NT) * scale + bias0 + col_penalty(pens)
        return s, v_ref[0, pl.ds(start, TQ), :]

    own = pl.multiple_of(i * TQ, TQ)
    s, vblk = tile(own, [jnp.where(row_sub > g, sel_pen(G * i + g), 0.0) for g in range(G)])
    s = jnp.where(c_i <= r_i, s, -jnp.inf)
    m0 = jnp.max(s, axis=1, keepdims=True)
    p = jnp.exp(s - m0)
    l0 = jnp.sum(p, axis=1, keepdims=True)
    acc0 = _dot(p.astype(BF16), vblk)

    def body(n, carry):
        m, l, acc = carry
        start = pl.multiple_of(n * TQ, TQ)
        off = (-slope) * ((i - n) * TQ).astype(F32)
        s, vblk = tile(start, [sel_pen(G * n + g) + off for g in range(G)])
        m_new = jnp.maximum(m, jnp.max(s, axis=1, keepdims=True))
        alpha = jnp.exp(m - m_new)
        p = jnp.exp(s - m_new)
        l = alpha * l + jnp.sum(p, axis=1, keepdims=True)
        acc = alpha * acc + _dot(p.astype(BF16), vblk)
        return m_new, l, acc

    _, l, acc = lax.fori_loop(0, i, body, (m0, l0, acc0))
    o_ref[0] = (acc / l).astype(o_ref.dtype)


def _moba(qkv, slopes, tq, cb0=0):
    B, S, _ = qkv.shape
    H, L, d = MOBA_HEADS, MOBA_BLOCK, MOBA_HEAD_DIM
    assert S % tq == 0 and tq % L == 0 and S // L <= LANES
    return pl.pallas_call(
        _moba_kernel,
        grid_spec=pltpu.PrefetchScalarGridSpec(
            num_scalar_prefetch=1,
            grid=(B, H, S // tq),
            in_specs=[
                pl.BlockSpec((1, tq, d), lambda b, h, i, sl: (b, i, cb0 + h)),
                pl.BlockSpec((1, S, d), lambda b, h, i, sl: (b, 0, cb0 + H + h)),
                pl.BlockSpec((1, S, d), lambda b, h, i, sl: (b, 0, cb0 + 2 * H + h)),
            ],
            out_specs=pl.BlockSpec((1, tq, d), lambda b, h, i, sl: (b, i, h)),
            scratch_shapes=[pltpu.VMEM((LANES, d), F32)],
        ),
        out_shape=jax.ShapeDtypeStruct((B, S, H * d), BF16),
        compiler_params=_params(("parallel", "parallel", "arbitrary")),
        name="moba",
    )(slopes, qkv, qkv, qkv)


def _fox_prep_kernel(f_ref, bf_ref, c_ref, carry_ref):
    ts = f_ref.shape[1]

    @pl.when(pl.program_id(1) == 0)
    def _():
        carry_ref[...] = jnp.zeros_like(carry_ref)

    z = f_ref[0] + bf_ref[...]
    lf = -_softplus(-z)
    tri = (lax.broadcasted_iota(I32, (ts, ts), 0) >= lax.broadcasted_iota(I32, (ts, ts), 1)).astype(BF16)
    c = _dot_exact_lhs(tri, lf) + carry_ref[0:1, :]
    c_ref[0] = c
    carry_ref[0:1, :] = c[ts - 1:ts, :]


def _fox_prep(f_logit, b_f, ts, cb=0):
    B, S, _ = f_logit.shape
    W = LANES
    return pl.pallas_call(
        _fox_prep_kernel,
        grid=(B, S // ts),
        in_specs=[pl.BlockSpec((1, ts, W), lambda b, j: (b, j, cb)), pl.BlockSpec((1, W), lambda b, j: (0, 0))],
        out_specs=pl.BlockSpec((1, ts, W), lambda b, j: (b, j, 0)),
        out_shape=jax.ShapeDtypeStruct((B, S, W), F32),
        scratch_shapes=[pltpu.VMEM((8, W), F32)],
        compiler_params=_params(("parallel", "arbitrary")),
        name="fox_prep",
    )(f_logit, b_f)


def _fox_kernel(q_ref, k_ref, v_ref, ccol_ref, crow_ref, o_ref):
    tq = q_ref.shape[1]
    i = pl.program_id(2)
    scale = FOX_HEAD_DIM ** -0.5
    q = q_ref[0]
    head_lane = lax.broadcasted_iota(I32, (tq, LANES), 1) == pl.program_id(1)
    cq = jnp.sum(jnp.where(head_lane, ccol_ref[0], 0.0), axis=1, keepdims=True)
    rc = lax.broadcasted_iota(I32, (tq, tq), 0) - lax.broadcasted_iota(I32, (tq, tq), 1)

    def scores(n):
        start = pl.multiple_of(n * tq, tq)
        s = _dot(q, k_ref[0, pl.ds(start, tq), :], _NT) * scale + cq - crow_ref[0, 0, n]
        return s, v_ref[0, pl.ds(start, tq), :]

    s, vblk = scores(i)
    s = jnp.where(rc >= 0, s, -jnp.inf)
    m0 = jnp.max(s, axis=1, keepdims=True)
    p = jnp.exp(s - m0)
    l0 = jnp.sum(p, axis=1, keepdims=True)
    acc0 = _dot(p.astype(BF16), vblk)

    def body(n, carry):
        m, l, acc = carry
        s, vblk = scores(n)
        m_new = jnp.maximum(m, jnp.max(s, axis=1, keepdims=True))
        alpha = jnp.exp(m - m_new)
        p = jnp.exp(s - m_new)
        l = alpha * l + jnp.sum(p, axis=1, keepdims=True)
        acc = alpha * acc + _dot(p.astype(BF16), vblk)
        return m_new, l, acc

    _, l, acc = lax.fori_loop(0, i, body, (m0, l0, acc0))
    o_ref[0] = (acc / l).astype(o_ref.dtype)


def _fox(qkv, c, tq, cb0=0):
    B, S, _ = qkv.shape
    H, d = FOX_HEADS, FOX_HEAD_DIM
    nq = S // tq
    c_row = jnp.transpose(c[:, :, :H], (0, 2, 1)).reshape(B, H, nq, 1, tq)
    return pl.pallas_call(
        _fox_kernel,
        grid=(B, H, nq),
        in_specs=[
            pl.BlockSpec((1, tq, d), lambda b, h, i: (b, i, cb0 + h)),
            pl.BlockSpec((1, S, d), lambda b, h, i: (b, 0, cb0 + H + h)),
            pl.BlockSpec((1, S, d), lambda b, h, i: (b, 0, cb0 + 2 * H + h)),
            pl.BlockSpec((1, tq, LANES), lambda b, h, i: (b, i, 0)),
            pl.BlockSpec((1, 1, nq, 1, tq), lambda b, h, i: (b, h, 0, 0, 0)),
        ],
        out_specs=pl.BlockSpec((1, tq, d), lambda b, h, i: (b, i, h)),
        out_shape=jax.ShapeDtypeStruct((B, S, H * d), BF16),
        compiler_params=_params(("parallel", "parallel", "arbitrary")),
        name="fox",
    )(qkv, qkv, qkv, c, c_row)


def _ret_kernel(q_ref, k_ref, v_ref, g_ref, lng_ref, lnb_ref, o_ref, state_ref):
    C = q_ref.shape[1]
    dk, dv = RET_KEY_DIM, RET_VALUE_DIM

    @pl.when(pl.program_id(1) == 0)
    def _():
        state_ref[...] = jnp.zeros_like(state_ref)

    row = lax.broadcasted_iota(I32, (C, C), 0)
    col = lax.broadcasted_iota(I32, (C, C), 1)
    diff = (row - col).astype(F32)
    tcol = lax.broadcasted_iota(I32, (C, 1), 0).astype(F32)
    scale = dk ** -0.5
    for h in range(RET_HEADS):
        log_gamma = float(np.log(1.0 - 2.0 ** (-5.0 - h)))
        intra = jnp.where(diff >= 0, jnp.exp(log_gamma * jnp.maximum(diff, 0.0)), 0.0)
        q_dec = jnp.exp(log_gamma * (tcol + 1.0))
        k_dec = jnp.exp(log_gamma * (C - 1.0 - tcol))
        blk_dec = float(np.exp(log_gamma * C))
        qh = q_ref[0, :, h * dk:(h + 1) * dk]
        kh = (k_ref[0, :, h * dk:(h + 1) * dk].astype(F32) * scale).astype(BF16)
        vh = v_ref[0, :, h * dv:(h + 1) * dv]
        R = state_ref[h]
        inner = _dot(qh, kh, _NT) * intra
        o = _dot(inner.astype(BF16), vh) + _dot(qh, R.astype(BF16)) * q_dec
        state_ref[h] = R * blk_dec + _dot((kh.astype(F32) * k_dec).astype(BF16), vh, _TN)
        mu = jnp.mean(o, axis=1, keepdims=True)
        d = o - mu
        var = jnp.mean(d * d, axis=1, keepdims=True)
        y = d * lax.rsqrt(var + RET_GN_EPS) * lng_ref[:, h * dv:(h + 1) * dv] + lnb_ref[:, h * dv:(h + 1) * dv]
        g = g_ref[0, :, h * dv:(h + 1) * dv].astype(F32)
        o_ref[0, :, h * dv:(h + 1) * dv] = (g * _sigmoid(g) * y).astype(o_ref.dtype)


def _retention(ret, ln_g, ln_b, C, col0=0):
    B, S, _ = ret.shape
    qk = RET_HEADS * RET_KEY_DIM
    assert col0 % RET_W == 0
    qb, vb = col0 // qk, col0 // RET_W
    return pl.pallas_call(
        _ret_kernel,
        grid=(B, S // C),
        in_specs=[
            pl.BlockSpec((1, C, qk), lambda b, c: (b, c, qb)),
            pl.BlockSpec((1, C, qk), lambda b, c: (b, c, qb + 1)),
            pl.BlockSpec((1, C, RET_W), lambda b, c: (b, c, vb + 1)),
            pl.BlockSpec((1, C, RET_W), lambda b, c: (b, c, vb + 2)),
            pl.BlockSpec((1, RET_W), lambda b, c: (0, 0)),
            pl.BlockSpec((1, RET_W), lambda b, c: (0, 0)),
        ],
        out_specs=pl.BlockSpec((1, C, RET_W), lambda b, c: (b, c, 0)),
        out_shape=jax.ShapeDtypeStruct((B, S, RET_W), BF16),
        scratch_shapes=[pltpu.VMEM((RET_HEADS, RET_KEY_DIM, RET_VALUE_DIM), F32)],
        compiler_params=_params(("parallel", "arbitrary")),
        name="retention",
    )(ret, ret, ret, ret, ln_g, ln_b)


def _rwkv_prep_kernel(x_ref, mu_ref, w0_ref, wup_ref, a0_ref, aup_ref, gup_ref, kk_ref, ka_ref, rk_ref, hsum_ref,
                      r_o, lw_o, k_o, v_o, a_o, b_o, g_o, bonus_o, carry_ref):
    ts = x_ref.shape[1]
    W = RWKV_W

    @pl.when(pl.program_id(1) == 0)
    def _():
        carry_ref[...] = jnp.zeros_like(carry_ref)

    x = x_ref[0]
    row = lax.broadcasted_iota(I32, x.shape, 0)
    shifted = jnp.where(row == 0, carry_ref[0:1, :], pltpu.roll(x, 1, 0))
    carry_ref[0:1, :] = x[ts - 1:ts, :]
    xs = x + (shifted - x) * mu_ref[...]
    r = xs[:, 0:W]
    k = xs[:, W:2 * W]
    v = xs[:, 2 * W:3 * W]
    wd = xs[:, 3 * W:3 * W + LANES]
    ad = xs[:, 3 * W + LANES:3 * W + 2 * LANES]
    gd = xs[:, 3 * W + 2 * LANES:3 * W + 3 * LANES]
    w_log = -_softplus(-(w0_ref[...] + _dot(jnp.tanh(wd), wup_ref[...], precision=HI))) - 0.5
    a = _sigmoid(a0_ref[...] + _dot(ad, aup_ref[...], precision=HI))
    g = _dot(_sigmoid(gd), gup_ref[...], precision=HI)
    kk = k * kk_ref[...]
    norm = jnp.sqrt(_dot_exact_rhs(kk * kk, hsum_ref[...]))
    kk = kk / jnp.maximum(norm, 1e-12)
    k2 = k * (1.0 + (a - 1.0) * ka_ref[...])
    r_o[0] = r
    lw_o[0] = -jnp.exp(w_log)
    k_o[0] = k2
    v_o[0] = v
    a_o[0] = -kk
    b_o[0] = kk * a
    g_o[0] = g
    bonus_o[0] = _dot_exact_rhs(r * k2 * rk_ref[...], hsum_ref[...]) * v


def _rwkv_prep(cols, mu, w0, wup, a0, aup, gup, k_k, k_a, r_k, hsum, ts):
    B, S, Wc = cols.shape
    W = RWKV_W
    vec = lambda n: pl.BlockSpec((1, n), lambda b, j: (0, 0))
    mat = lambda r, c: pl.BlockSpec((r, c), lambda b, j: (0, 0))
    out = jax.ShapeDtypeStruct((B, S, W), F32)
    return pl.pallas_call(
        _rwkv_prep_kernel,
        grid=(B, S // ts),
        in_specs=[pl.BlockSpec((1, ts, Wc), lambda b, j: (b, j, 0)), vec(Wc), vec(W), mat(LANES, W), vec(W),
                  mat(LANES, W), mat(LANES, W), vec(W), vec(W), vec(W), mat(W, W)],
        out_specs=[pl.BlockSpec((1, ts, W), lambda b, j: (b, j, 0))] * 8,
        out_shape=[out] * 8,
        scratch_shapes=[pltpu.VMEM((8, Wc), F32)],
        compiler_params=_params(("parallel", "arbitrary"), V7X_VMEM_LIMIT),
        name="rwkv_prep",
    )(cols, mu, w0, wup, a0, aup, gup, k_k, k_a, r_k, hsum)


def _rwkv_scan_kernel(r_ref, lw_ref, k_ref, v_ref, a_ref, b_ref, g_ref, bonus_ref, lng_ref, lnb_ref, hmean_ref,
                      o_ref, state_ref, y_ref):
    C = RWKV_CHUNK
    N = RWKV_HEAD_DIM
    TS = r_ref.shape[1]

    @pl.when(pl.program_id(1) == 0)
    def _():
        state_ref[...] = jnp.zeros_like(state_ref)

    row = lax.broadcasted_iota(I32, (C, C), 0)
    col = lax.broadcasted_iota(I32, (C, C), 1)
    incl = row >= col
    strict = row > col
    eye = (row == col).astype(F32)
    tri = incl.astype(BF16)

    def chunk(c, carry):
        sl = pl.ds(pl.multiple_of(c * C, C), C)
        lw = lw_ref[0, sl, :]
        cum = _dot_exact_lhs(tri, lw)
        cum_last = cum[C - 1:C, :]
        e_pos = jnp.exp(cum)
        e_neg = jnp.exp(-cum)
        e_last = jnp.exp(cum_last - cum)
        rt = r_ref[0, sl, :] * e_pos
        at = a_ref[0, sl, :] * jnp.exp(cum - lw)
        b = b_ref[0, sl, :]
        k = k_ref[0, sl, :]
        v = v_ref[0, sl, :]
        bt = b * e_neg
        kt = k * e_neg
        bl = b * e_last
        kl = k * e_last
        pc = jnp.exp(cum_last)
        heads = range(RWKV_HEADS)
        sl_h = [slice(h * N, (h + 1) * N) for h in heads]
        at_h = [at[:, s].astype(BF16) for s in sl_h]
        rt_h = [rt[:, s] for s in sl_h]
        v_h = [v[:, s].astype(BF16) for s in sl_h]
        bt_h = [bt[:, s].astype(BF16) for s in sl_h]
        kt_h = [kt[:, s].astype(BF16) for s in sl_h]
        bl_h = [bl[:, s].astype(BF16) for s in sl_h]
        kl_h = [kl[:, s].astype(BF16) for s in sl_h]
        lhs2 = [jnp.concatenate([at_h[h], rt_h[h].astype(BF16)], axis=0) for h in heads]
        sb = [_dot(lhs2[h], bt_h[h], _NT) for h in heads]
        sk = [_dot(lhs2[h], kt_h[h], _NT) for h in heads]
        a_ab = [jnp.where(strict, sb[h][:C], 0.0) for h in heads]
        m_rb = [jnp.where(incl, sb[h][C:], 0.0).astype(BF16) for h in heads]
        a_ak = [jnp.where(strict, sk[h][:C], 0.0).astype(BF16) for h in heads]
        m_rk = [jnp.where(incl, sk[h][C:], 0.0).astype(BF16) for h in heads]
        xb = [a_ab[h].astype(BF16) for h in heads]
        t_inv = [eye + a_ab[h] for h in heads]
        akv = [_dot(a_ak[h], v_h[h]).astype(BF16) for h in heads]
        mrkv = [_dot(m_rk[h], v_h[h]) for h in heads]
        klv = [_dot(kl_h[h], v_h[h], _TN) for h in heads]
        x = [_dot(xb[h], xb[h]) for h in heads]
        for step in range(int(np.log2(C)) - 1):
            xb = [x[h].astype(BF16) for h in heads]
            if step < int(np.log2(C)) - 2:
                x = [_dot(xb[h], xb[h]) for h in heads]
            t_inv = [t_inv[h] + _dot(t_inv[h].astype(BF16), xb[h]) for h in heads]
        t_b = [t_inv[h].astype(BF16) for h in heads]
        abar = [_dot(t_b[h], at_h[h]).astype(BF16) for h in heads]
        u0 = [_dot(t_b[h], akv[h]).astype(BF16) for h in heads]
        rbar = [(rt_h[h] + _dot(m_rb[h], abar[h])).astype(BF16) for h in heads]
        y0 = [_dot(m_rb[h], u0[h]) + mrkv[h] for h in heads]
        gmat = [(eye * pc[:, sl_h[h]] + _dot(bl_h[h], abar[h], _TN)).astype(BF16) for h in heads]
        hadd = [_dot(bl_h[h], u0[h], _TN) + klv[h] for h in heads]
        h0b = [state_ref[h].astype(BF16) for h in heads]
        for h in heads:
            y_ref[:, sl_h[h]] = _dot(rbar[h], h0b[h]) + y0[h]
        for h in heads:
            state_ref[h] = _dot(gmat[h], h0b[h]) + hadd[h]
        y = y_ref[...]
        mu = _dot_exact_rhs(y, hmean_ref[...])
        d = y - mu
        var = _dot_exact_rhs(d * d, hmean_ref[...])
        yn = d * lax.rsqrt(var + RWKV_GN_EPS) * lng_ref[...] + lnb_ref[...]
        o_ref[0, sl, :] = ((yn + bonus_ref[0, sl, :]) * g_ref[0, sl, :]).astype(o_ref.dtype)
        return carry

    lax.fori_loop(0, TS // C, chunk, 0)


def _rwkv_scan(r, lw, k, v, a, b, g, bonus, ln_g, ln_b, hmean, ts):
    B, S, W = r.shape
    blk = pl.BlockSpec((1, ts, W), lambda bb, j: (bb, j, 0))
    vec = pl.BlockSpec((1, W), lambda bb, j: (0, 0))
    return pl.pallas_call(
        _rwkv_scan_kernel,
        grid=(B, S // ts),
        in_specs=[blk] * 8 + [vec, vec, pl.BlockSpec((W, W), lambda bb, j: (0, 0))],
        out_specs=blk,
        out_shape=jax.ShapeDtypeStruct((B, S, W), BF16),
        scratch_shapes=[pltpu.VMEM((RWKV_HEADS, RWKV_HEAD_DIM, RWKV_HEAD_DIM), F32),
                        pltpu.VMEM((RWKV_CHUNK, W), F32)],
        compiler_params=_params(("parallel", "arbitrary"), V7X_VMEM_LIMIT),
        name="rwkv_scan",
    )(r, lw, k, v, a, b, g, bonus, ln_g, ln_b, hmean)


def _layer_norm_rows(z, g, b):
    mu = jnp.mean(z, axis=1, keepdims=True)
    d = z - mu
    var = jnp.mean(d * d, axis=1, keepdims=True)
    return d * lax.rsqrt(var + LN_EPS) * g + b


def _pack_bf16_pairs(lo, hi):
    bits = lambda a: pltpu.bitcast(a.astype(BF16).astype(F32), U32)
    return (bits(lo) >> 16) | (bits(hi) & jnp.uint32(0xFFFF0000))


def _outproj_kernel(ya_ref, yb_ref, yc_ref, yd_ref, h_ref, w_ref, g_ref, b_ref, rw_ref, rb_ref,
                    h1_ref, hp_ref, ti_ref, tg_ref):
    mix = _dot(ya_ref[...], w_ref[0]) + _dot(yb_ref[...], w_ref[1]) + _dot(yc_ref[...], w_ref[2]) + _dot(yd_ref[...], w_ref[3])
    hn = _layer_norm_rows(DEEPNORM_ALPHA * h_ref[...] + mix, g_ref[...], b_ref[...])
    h1_ref[...] = hn
    hp_ref[...] = _pack_bf16_pairs(hn[:, :hn.shape[1] // 2], hn[:, hn.shape[1] // 2:])
    tm = hn.shape[0]
    lane = lax.broadcasted_iota(I32, (tm, LANES), 1)
    lane_f = lane.astype(F32)
    hn_hi = hn.astype(BF16)
    hn_lo = (hn - hn_hi.astype(F32)).astype(BF16)
    both = _dot(hn_hi, rw_ref[...])
    logits = both[:, :LANES] + (both[:, LANES:] + _dot(hn_lo, rw_ref[:, :LANES])) + rb_ref[...]
    logits = jnp.where(lane < N_EXPERTS, logits, -jnp.inf)
    top = _topk_lanes(logits, lane_f, TOP_K)
    v0 = top[0][0]
    es = [jnp.exp(vk - v0) for vk, _ in top]
    denom = es[0] + es[1] + es[2] + es[3]
    ti = jnp.zeros((tm, LANES), F32)
    tg = jnp.zeros((tm, LANES), F32)
    for kk in range(TOP_K):
        ti = jnp.where(lane == kk, top[kk][1], ti)
        tg = jnp.where(lane == kk, es[kk] / denom, tg)
    ti_ref[...] = ti.astype(I32)
    tg_ref[...] = tg


def _outproj_ln_router(ys, h, w_out4, ln_g, ln_b, rw, rb, tm):
    T, D = h.shape
    Wg = ys[0].shape[1]
    yspec = pl.BlockSpec((tm, Wg), lambda i: (i, 0))
    row = pl.BlockSpec((tm, D), lambda i: (i, 0))
    vec = lambda n: pl.BlockSpec((1, n), lambda i: (0, 0))
    small = pl.BlockSpec((tm, LANES), lambda i: (i, 0))
    return pl.pallas_call(
        _outproj_kernel,
        grid=(T // tm,),
        in_specs=[yspec] * 4 + [row, pl.BlockSpec((4, Wg, D), lambda i: (0, 0, 0)), vec(D), vec(D),
                                pl.BlockSpec((D, 2 * LANES), lambda i: (0, 0)), vec(LANES)],
        out_specs=[row, pl.BlockSpec((tm, D // 2), lambda i: (i, 0)), small, small],
        out_shape=[jax.ShapeDtypeStruct((T, D), F32), jax.ShapeDtypeStruct((T, D // 2), U32),
                   jax.ShapeDtypeStruct((T, LANES), I32), jax.ShapeDtypeStruct((T, LANES), F32)],
        compiler_params=_params(("parallel",), V7X_VMEM_LIMIT),
        name="outproj_ln_router",
    )(*ys, h, w_out4, ln_g, ln_b, rw, rb)


def _unpack_bf16_pairs(w):
    return pltpu.bitcast(w << 16, F32), pltpu.bitcast(w & jnp.uint32(0xFFFF0000), F32)


def _moe_kernel(te_ref, nu_ref, src0_ref, srcn_ref, dstp_ref, hp_hbm, w1g_ref, w1l_ref, b1g_ref, b1l_ref, w2a_ref,
                w2b_ref, b2a_ref, b2b_ref, y_hbm, xu_ref, xb_ref, act_ref, acc_ref, gsem, ssem, *, n_real_rows):
    i = pl.program_id(0)
    j = pl.program_id(1)
    ns = MOE_STEPS
    nj = 2 * ns
    tm, half = xu_ref.shape
    rows_per_step = tm // nj
    th = act_ref.shape[1] // ns
    tn = half // ns
    nu = nu_ref[0]
    valid = i < nu
    slot = i % 2
    base = j * rows_per_step

    def wait_gather():
        pltpu.make_async_copy(hp_hbm.at[pl.ds(0, tm), :], xu_ref, gsem).wait()

    def wait_scatter():
        pltpu.make_async_copy(acc_ref.at[0], y_hbm.at[pl.ds(0, tm), :], ssem).wait()

    def issue_gather_chunk():
        for r in range(rows_per_step):
            t = srcn_ref[0, 0, base + r]
            pltpu.make_async_copy(hp_hbm.at[pl.ds(t, 1), :], xu_ref.at[pl.ds(base + r, 1), :], gsem).start()

    def issue_scatter_chunk():
        for r in range(rows_per_step):
            t = dstp_ref[0, 0, base + r]
            pltpu.make_async_copy(acc_ref.at[1 - slot, pl.ds(base + r, 1), :], y_hbm.at[pl.ds(t, 1), :], ssem).start()

    @pl.when((i == 0) & (j == 0))
    def _():
        acc_ref[1] = jnp.zeros(acc_ref.shape[1:], U32)
        fills = [pltpu.make_async_copy(acc_ref.at[1], y_hbm.at[pl.ds(n_real_rows + f * tm, tm), :], ssem)
                 for f in range((y_hbm.shape[0] - n_real_rows) // tm)]
        for fill in fills:
            fill.start()
        for fill in fills:
            fill.wait()

        def issue(r, carry):
            t = src0_ref[0, 0, r]
            pltpu.make_async_copy(hp_hbm.at[pl.ds(t, 1), :], xu_ref.at[pl.ds(r, 1), :], gsem).start()
            return carry

        lax.fori_loop(0, tm, issue, 0)

    @pl.when((j == 0) & (i <= nu))
    def _():
        wait_gather()

    @pl.when(valid & (j == 0))
    def _():
        lo, hi = _unpack_bf16_pairs(xu_ref[...])
        xb_ref[:, :half] = lo.astype(BF16)
        xb_ref[:, half:] = hi.astype(BF16)

    @pl.when(valid & (j < ns))
    def _():
        issue_gather_chunk()
        issue_scatter_chunk()
        xb = xb_ref[...]
        w1g = w1g_ref[0].astype(BF16)
        w1l = w1l_ref[0].astype(BF16)
        b1g = b1g_ref[0]
        b1l = b1l_ref[0]
        acts = []
        for c in range(2):
            cs = slice(c * (th // 2), (c + 1) * (th // 2))
            glu = jnp.minimum(_dot(xb, w1g[:, cs]) + b1g[:, cs], SWIGLU_LIMIT)
            lin = jnp.clip(_dot(xb, w1l[:, cs]) + b1l[:, cs], -SWIGLU_LIMIT, SWIGLU_LIMIT)
            acts.append((glu * _sigmoid(SWIGLU_ALPHA * glu) * (lin + 1.0)).astype(BF16))
        act = jnp.concatenate(acts, axis=1)
        for jj in range(ns):
            @pl.when(j == jj)
            def _():
                act_ref[:, jj * th:(jj + 1) * th] = act

    @pl.when(valid & (j >= ns))
    def _():
        act = act_ref[...]
        lo = _dot(act, w2a_ref[0].astype(BF16)) + b2a_ref[0]
        issue_gather_chunk()
        hi = _dot(act, w2b_ref[0].astype(BF16)) + b2b_ref[0]
        issue_scatter_chunk()
        packed = _pack_bf16_pairs(lo, hi)
        for nn in range(ns):
            @pl.when(j == ns + nn)
            def _():
                acc_ref[slot, :, nn * tn:(nn + 1) * tn] = packed

        @pl.when(j == nj - 1)
        def _():
            wait_scatter()

    @pl.when(i == nu)
    def _():
        issue_scatter_chunk()

        @pl.when(j == nj - 1)
        def _():
            wait_scatter()


def _moe_ffn(hp, src_tok, dst_row, tile_expert, n_used, w1, b1, w2, b2, tm):
    NT = src_tok.shape[0]
    _, D, _ = w1.shape
    ns = MOE_STEPS
    nj = 2 * ns
    half = D // 2
    th = D_EXPERT // ns
    tn = half // ns
    assert tm % nj == 0

    def step(i, j, nu):
        return jnp.where(i < nu[0], j, nj - 1)

    def j1(i, j, te, nu):
        return jnp.minimum(step(i, j, nu), ns - 1)

    def j2(i, j, te, nu):
        return jnp.maximum(step(i, j, nu) - ns, 0)

    smem = lambda f: pl.BlockSpec((1, 1, tm), f, memory_space=pltpu.SMEM)
    assert (NT * tm - hp.shape[0] * TOP_K) % tm == 0
    return pl.pallas_call(
        functools.partial(_moe_kernel, n_real_rows=hp.shape[0] * TOP_K),
        grid_spec=pltpu.PrefetchScalarGridSpec(
            num_scalar_prefetch=2,
            grid=(NT, nj),
            in_specs=[
                smem(lambda i, j, te, nu: (0, 0, 0)),
                smem(lambda i, j, te, nu: (jnp.minimum(i + 1, NT - 1), 0, 0)),
                smem(lambda i, j, te, nu: (jnp.maximum(i - 1, 0), 0, 0)),
                pl.BlockSpec(memory_space=pl.ANY),
                pl.BlockSpec((1, D, th), lambda i, j, te, nu: (te[i], 0, j1(i, j, te, nu))),
                pl.BlockSpec((1, D, th), lambda i, j, te, nu: (te[i], 0, ns + j1(i, j, te, nu))),
                pl.BlockSpec((1, 1, th), lambda i, j, te, nu: (te[i], 0, j1(i, j, te, nu))),
                pl.BlockSpec((1, 1, th), lambda i, j, te, nu: (te[i], 0, ns + j1(i, j, te, nu))),
                pl.BlockSpec((1, D_EXPERT, tn), lambda i, j, te, nu: (te[i], 0, j2(i, j, te, nu))),
                pl.BlockSpec((1, D_EXPERT, tn), lambda i, j, te, nu: (te[i], 0, ns + j2(i, j, te, nu))),
                pl.BlockSpec((1, 1, tn), lambda i, j, te, nu: (te[i], 0, j2(i, j, te, nu))),
                pl.BlockSpec((1, 1, tn), lambda i, j, te, nu: (te[i], 0, ns + j2(i, j, te, nu))),
            ],
            out_specs=pl.BlockSpec(memory_space=pl.ANY),
            scratch_shapes=[pltpu.VMEM((tm, half), U32), pltpu.VMEM((tm, D), BF16), pltpu.VMEM((tm, D_EXPERT), BF16),
                            pltpu.VMEM((2, tm, half), U32), pltpu.SemaphoreType.DMA, pltpu.SemaphoreType.DMA],
        ),
        out_shape=jax.ShapeDtypeStruct((NT * tm, half), U32),
        compiler_params=_params(("arbitrary", "arbitrary"), V7X_VMEM_LIMIT_MOE),
        name="moe_ffn",
    )(tile_expert, n_used, src_tok, src_tok, dst_row, hp, w1, w1, b1, b1, w2, w2, b2, b2)


def _combine_kernel(y0_ref, y1_ref, y2_ref, y3_ref, tg_ref, h_ref, g_ref, b_ref, o_ref):
    tg = tg_ref[...]
    lo = hi = None
    for kk, y_ref in enumerate((y0_ref, y1_ref, y2_ref, y3_ref)):
        ylo, yhi = _unpack_bf16_pairs(y_ref[...])
        gk = tg[:, kk:kk + 1]
        lo = gk * ylo if lo is None else lo + gk * ylo
        hi = gk * yhi if hi is None else hi + gk * yhi
    ffn = jnp.concatenate([lo, hi], axis=1)
    o_ref[...] = _layer_norm_rows(DEEPNORM_ALPHA * h_ref[...] + ffn, g_ref[...], b_ref[...])


def _combine_ln(y, tg, h1, ln_g, ln_b, tq):
    T, D = h1.shape
    row = pl.BlockSpec((tq, D), lambda i: (i, 0))
    vec = pl.BlockSpec((1, D), lambda i: (0, 0))
    nq = T // tq
    ys = [pl.BlockSpec((tq, D // 2), functools.partial(lambda i, kk: (kk * nq + i, 0), kk=kk)) for kk in range(TOP_K)]
    return pl.pallas_call(
        _combine_kernel,
        grid=(nq,),
        in_specs=ys + [pl.BlockSpec((tq, LANES), lambda i: (i, 0)), row, vec, vec],
        out_specs=row,
        out_shape=jax.ShapeDtypeStruct((T, D), F32),
        compiler_params=_params(("parallel",), V7X_VMEM_LIMIT),
        name="combine_ln",
    )(y, y, y, y, tg, h1, ln_g, ln_b)


def _dispatch_tables(top_idx, tm, n_tiles):
    T = top_idx.shape[0]
    P = T * TOP_K
    E = N_EXPERTS
    R = n_tiles * tm
    shift = int(np.ceil(np.log2(R)))
    assert (E + 1) << shift < 2 ** 31
    e = top_idx.reshape(P)
    experts = jnp.arange(E, dtype=I32)
    counts = jnp.sum((e[:, None] == experts[None, :]).astype(I32), axis=0)
    tiles = (counts + tm - 1) // tm
    tile_end = jnp.cumsum(tiles)
    n_used = tile_end[-1]
    pad_end = jnp.cumsum(tiles * tm - counts)
    d = jnp.arange(R - P, dtype=I32)
    pad_expert = jnp.sum((pad_end[None, :] <= d[:, None]).astype(I32), axis=1)
    keys = jnp.concatenate([(e << shift) + jnp.arange(P, dtype=I32), (pad_expert << shift) + P + d])
    ids = jnp.sort(keys) & ((1 << shift) - 1)
    real = ids < P
    src_tok = jnp.where(real, ids // TOP_K, 0).astype(I32)
    dst_row = jnp.where(real, (ids % TOP_K) * T + ids // TOP_K, ids).astype(I32)
    tile_ids = jnp.arange(n_tiles, dtype=I32)
    last = jnp.maximum(n_used - 1, 0)
    te = jnp.sum((tile_end[None, :] <= jnp.minimum(tile_ids, last)[:, None]).astype(I32), axis=1)
    return src_tok, dst_row, te.astype(I32), n_used.reshape(1).astype(I32)


def _pad_cols(a, n):
    return jnp.pad(a, ((0, 0), (0, n - a.shape[1])))


def _pad_rows(a, n):
    return jnp.pad(a, ((0, n - a.shape[0]), (0, 0)))


def _rwkv_col_layout(a, tail=None):
    W = RWKV_W
    rkv, wd, ad, gd = a[:, :3 * W], a[:, 3 * W:3 * W + 64], a[:, 3 * W + 64:3 * W + 128], a[:, 3 * W + 128:]
    if tail is None:
        tail = jnp.zeros((a.shape[0], LANES), a.dtype)
    return jnp.concatenate([rkv, _pad_cols(wd, LANES), _pad_cols(ad, LANES), gd, tail], axis=1)


def _layer(layer, h, B, S, w_in, mu, w0, w_up, a0, a_up, g_up, k_k, k_a, r_k, rln_g, rln_b, fox_b_f, ret_ln_g, ret_ln_b,
           w_out, ln1_g, ln1_b, router_w, router_b, w1, b1, w2, b2, ln2_g, ln2_b):
    T, D = h.shape
    w_att, w_f32 = _prep_in_weights(w_in, layer)
    fox_cb0 = MOBA_COLS // FOX_HEAD_DIM
    ret_col0 = MOBA_COLS + 3 * FOX_W
    foxf_cb = w_f32.shape[1] // LANES - 1

    tm = min(512, T)
    att = _matmul(h, w_att, BF16, tm, MOBA_COLS).reshape(B, S, -1)
    rwkv_cols = _matmul(h, w_f32, F32, tm, 1024).reshape(B, S, -1)

    slopes = jnp.exp2(-8.0 * jnp.arange(1, MOBA_HEADS + 1, dtype=F32) / MOBA_HEADS)
    y_a = _moba(att, slopes, min(512, S))

    head_of = np.arange(RWKV_W) // RWKV_HEAD_DIM
    same = (head_of[:, None] == head_of[None, :]).astype(np.float32)
    hsum = jnp.asarray(same, BF16)
    hmean = jnp.asarray(same / RWKV_HEAD_DIM, BF16)
    row = lambda a: a.reshape(1, -1)
    ts_r = min(256, S)
    prep = _rwkv_prep(rwkv_cols, _rwkv_col_layout(row(mu)), row(w0), _pad_rows(w_up, LANES), row(a0),
                      _pad_rows(a_up, LANES), g_up, row(k_k), row(k_a), row(r_k), hsum, ts_r)
    y_b = _rwkv_scan(*prep, row(rln_g), row(rln_b), hmean, min(512, S))

    c = _fox_prep(rwkv_cols, _pad_cols(row(fox_b_f), LANES), min(512, S), foxf_cb)
    y_c = _fox(att, c, min(512, S), fox_cb0)

    y_d = _retention(att, row(ret_ln_g), row(ret_ln_b), min(256, S), ret_col0)

    Wg = D // 4
    ys = [y.reshape(T, Wg) for y in (y_a, y_b, y_c, y_d)]
    rw32 = _pad_cols(router_w, LANES)
    rw_hi = rw32.astype(BF16)
    rw = jnp.concatenate([rw_hi, (rw32 - rw_hi.astype(F32)).astype(BF16)], axis=1)
    rb = _pad_cols(row(router_b), LANES)
    h1, hp, top_i, top_g = _outproj_ln_router(ys, h, w_out.astype(BF16).reshape(4, Wg, D), row(ln1_g), row(ln1_b),
                                              rw, rb, min(256, T))

    tm_e = min(MOE_TILE_ROWS, T)
    n_tiles = (T * TOP_K) // tm_e + N_EXPERTS + 1
    src_tok, dst_row, te, n_used = _dispatch_tables(top_i[:, :TOP_K], tm_e, n_tiles)
    y = _moe_ffn(hp, src_tok.reshape(n_tiles, 1, tm_e), dst_row.reshape(n_tiles, 1, tm_e), te + layer * N_EXPERTS,
                 n_used, w1, b1, w2, b2, tm_e)
    return _combine_ln(y, top_g, h1, row(ln2_g), row(ln2_b), min(256, T))


def kernel(x, w_in, rwkv_mu, rwkv_w0, rwkv_w_up, rwkv_a0, rwkv_a_up, rwkv_g_up, rwkv_k_k, rwkv_k_a, rwkv_r_k, rwkv_ln_g, rwkv_ln_b, fox_b_f, ret_ln_g, ret_ln_b, w_out, ln1_g, ln1_b, router_w, router_b, exp_w1, exp_b1, exp_w2, exp_b2, ln2_g, ln2_b):
    B, S, D = x.shape
    h = x.reshape(B * S, D)
    LE = exp_w1.shape[0] * exp_w1.shape[1]
    w1 = exp_w1.reshape(LE, D, -1)
    b1 = exp_b1.reshape(LE, 1, -1)
    w2 = exp_w2.reshape(LE, -1, D)
    b2 = exp_b2.reshape(LE, 1, D)
    for l in range(DEPTH):
        h = _layer(l, h, B, S, w_in, rwkv_mu[l], rwkv_w0[l], rwkv_w_up[l], rwkv_a0[l], rwkv_a_up[l], rwkv_g_up[l],
                   rwkv_k_k[l], rwkv_k_a[l], rwkv_r_k[l].reshape(-1), rwkv_ln_g[l], rwkv_ln_b[l], fox_b_f[l],
                   ret_ln_g[l], ret_ln_b[l], w_out[l], ln1_g[l], ln1_b[l], router_w[l], router_b[l],
                   w1, b1, w2, b2, ln2_g[l], ln2_b[l])
    return h.reshape(B, S, D)
```
